```python
import math
import jax, jax.numpy as jnp
from jax import lax
import numpy as np

D_MODEL = 2048
BATCH = 4
SEQ = 4096
DEPTH = 4

CHUNK = 64
N_MIXERS = 2
BRANCH = D_MODEL
CONV_K = 3
SB_HEADS = 16
SB_HEAD_DIM = BRANCH // SB_HEADS
Q_BLOCK = 128
RMS_EPS = 1e-6

kernel_name = "hybrid_shortconv_stickbreaking_trunk"


def rmsnorm(x, g):
    xf = x.astype(jnp.float32)
    y = xf * lax.rsqrt(jnp.mean(xf * xf, axis=-1, keepdims=True) + RMS_EPS)
    return (y * g.astype(jnp.float32)).astype(x.dtype)


def causal_depthwise_conv(u, w):
    rhs = w[:, None, :]
    return lax.conv_general_dilated(
        u, rhs.astype(u.dtype), window_strides=(1,), padding=[(CONV_K - 1, 0)],
        dimension_numbers=("NWC", "WIO", "NWC"), feature_group_count=u.shape[-1])


def short_conv_mixer(h, w_in, conv_w, w_out):
    proj = jnp.einsum("bsd,de->bse", h, w_in)
    b_gate, c_gate, xt, z = jnp.split(proj, 4, axis=-1)
    y = b_gate * causal_depthwise_conv(c_gate * xt, conv_w)
    return jnp.einsum("bse,ed->bsd", jax.nn.silu(z) * y, w_out)


def stick_breaking_mixer(h, w_in, w_out):
    bsz, seq, _ = h.shape
    proj = jnp.einsum("bsd,de->bse", h, w_in)
    q, k, v, z = jnp.split(proj, 4, axis=-1)
    to_heads = lambda t: t.reshape(bsz, seq, SB_HEADS, SB_HEAD_DIM).transpose(0, 2, 1, 3)
    q, k, v = to_heads(q), to_heads(k), to_heads(v)
    scale = 1.0 / math.sqrt(SB_HEAD_DIM)
    outs = []
    for blk in range(seq // Q_BLOCK):
        s0 = blk * Q_BLOCK
        end = s0 + Q_BLOCK
        qb = q[:, :, s0:end]
        kb = k[:, :, :end]
        vb = v[:, :, :end]
        logits = jnp.einsum("bhqd,bhkd->bhqk", qb, kb).astype(jnp.float32) * scale
        t_idx = s0 + jnp.arange(Q_BLOCK)[:, None]
        s_idx = jnp.arange(end)[None, :]
        mask = s_idx < t_idx
        log_keep = jnp.where(mask, jax.nn.log_sigmoid(-logits), 0.0)
        tail = lax.cumsum(log_keep, axis=3, reverse=True) - log_keep
        weights = jnp.where(mask, jnp.exp(jax.nn.log_sigmoid(logits) + tail), 0.0)
        outs.append(jnp.einsum("bhqk,bhkd->bhqd", weights.astype(vb.dtype), vb))
    o = jnp.concatenate(outs, axis=2)
    o = o.transpose(0, 2, 1, 3).reshape(bsz, seq, BRANCH)
    return jnp.einsum("bse,ed->bsd", jax.nn.silu(z) * o, w_out)


def setup_inputs(seed: int = 0) -> dict:
    key = jax.random.key(seed)
    keys = iter(jax.random.split(key, 64))
    nrm = lambda shape, s: jax.random.normal(next(keys), shape, jnp.float32) * s
    gain = lambda: 1.0 + nrm((D_MODEL,), 0.02)
    inp = {"x": nrm((BATCH, SEQ, D_MODEL), 1.0)}
    for i in range(DEPTH):
        inp[f"ln_pre_{i}"] = gain()
        if i % N_MIXERS == 0:
            inp[f"conv_w_in_{i}"] = nrm((D_MODEL, 4 * BRANCH), D_MODEL ** -0.5)
            inp[f"conv_w_{i}"] = nrm((CONV_K, BRANCH), CONV_K ** -0.5)
            inp[f"conv_w_out_{i}"] = nrm((BRANCH, D_MODEL), BRANCH ** -0.5)
        else:
            inp[f"sb_w_in_{i}"] = nrm((D_MODEL, 4 * BRANCH), D_MODEL ** -0.5)
            inp[f"sb_w_out_{i}"] = nrm((BRANCH, D_MODEL), BRANCH ** -0.5)
        inp[f"ln_post_{i}"] = gain()
    return inp


def reference(x,
              ln_pre_0, conv_w_in_0, conv_w_0, conv_w_out_0, ln_post_0,
              ln_pre_1, sb_w_in_1, sb_w_out_1, ln_post_1,
              ln_pre_2, conv_w_in_2, conv_w_2, conv_w_out_2, ln_post_2,
              ln_pre_3, sb_w_in_3, sb_w_out_3, ln_post_3):
    layers = [
        (ln_pre_0, (conv_w_in_0, conv_w_0, conv_w_out_0), ln_post_0),
        (ln_pre_1, (sb_w_in_1, sb_w_out_1), ln_post_1),
        (ln_pre_2, (conv_w_in_2, conv_w_2, conv_w_out_2), ln_post_2),
        (ln_pre_3, (sb_w_in_3, sb_w_out_3), ln_post_3),
    ]
    h = x
    for i in range(DEPTH):
        g_pre, params, g_post = layers[i]
        u = rmsnorm(h, g_pre)
        if i % N_MIXERS == 0:
            m = short_conv_mixer(u, *params)
        else:
            m = stick_breaking_mixer(u, *params)
        h = h + rmsnorm(m, g_post)
    return h
```

```python
import functools
import math

import jax
import jax.numpy as jnp
from jax import lax
from jax.experimental import pallas as pl
from jax.experimental.pallas import tpu as pltpu

D_MODEL = 2048
BRANCH = D_MODEL
CONV_K = 3
SB_HEADS = 16
SB_HEAD_DIM = BRANCH // SB_HEADS
RMS_EPS = 1e-6

BF16_ROW_TILE = 16
VMEM_LIMIT_BYTES = 56 * 1024 * 1024

F32 = jnp.float32
BF16 = jnp.bfloat16


def _params(*semantics):
    return pltpu.CompilerParams(dimension_semantics=semantics,
                                vmem_limit_bytes=VMEM_LIMIT_BYTES)


def _rms(x, g):
    ms = jnp.mean(x * x, axis=-1, keepdims=True)
    return x * lax.rsqrt(ms + RMS_EPS) * g


def _silu(z):
    return z * jax.nn.sigmoid(z)


def _mm(a, w_ref):
    return jnp.dot(a, w_ref[...], preferred_element_type=F32)


def _prenorm_kernel(x_ref, g_ref, u_ref):
    u_ref[...] = _rms(x_ref[...], g_ref[...]).astype(u_ref.dtype)


def _prenorm(x, g, *, tm=512):
    m, d = x.shape
    return pl.pallas_call(
        _prenorm_kernel,
        grid=(m // tm,),
        in_specs=[pl.BlockSpec((tm, d), lambda i: (i, 0)),
                  pl.BlockSpec((1, d), lambda i: (0, 0))],
        out_specs=pl.BlockSpec((tm, d), lambda i: (i, 0)),
        out_shape=jax.ShapeDtypeStruct((m, d), BF16),
        compiler_params=_params("parallel"),
        name="prenorm",
    )(x, g)


def _conv_mixer_kernel(u_ref, uh_ref, wb_ref, wc_ref, wx_ref, wz_ref, cw_ref,
                       a_ref, *, tiles_per_seq):
    i = pl.program_id(0)
    u = u_ref[...]
    cx = _mm(u, wc_ref) * _mm(u, wx_ref)
    uh = uh_ref[...]
    cxh = _mm(uh, wc_ref) * _mm(uh, wx_ref)
    cxh = jnp.where(i % tiles_per_seq == 0, 0.0, cxh)
    full = jnp.concatenate([cxh, cx], axis=0)
    prev1 = pltpu.roll(full, 1, 0)[BF16_ROW_TILE:]
    prev2 = pltpu.roll(full, 2, 0)[BF16_ROW_TILE:]
    cw = cw_ref[...]
    conv = cw[2:3] * cx + cw[1:2] * prev1 + cw[0:1] * prev2
    y = _mm(u, wb_ref) * conv
    z = _mm(u, wz_ref)
    a_ref[...] = (_silu(z) * y).astype(a_ref.dtype)


def _conv_mixer(u, w_in, conv_w, *, seq, tm=1024, tn=256):
    m, d = u.shape
    nj = BRANCH // tn
    halo_blocks_per_tile = tm // BF16_ROW_TILE

    def w_spec(chunk):
        return pl.BlockSpec((d, tn), lambda i, j: (0, chunk * nj + j))

    return pl.pallas_call(
        functools.partial(_conv_mixer_kernel, tiles_per_seq=seq // tm),
        grid=(m // tm, nj),
        in_specs=[
            pl.BlockSpec((tm, d), lambda i, j: (i, 0)),
            pl.BlockSpec((BF16_ROW_TILE, d),
                         lambda i, j: (jnp.maximum(i * halo_blocks_per_tile - 1, 0), 0)),
            w_spec(0), w_spec(1), w_spec(2), w_spec(3),
            pl.BlockSpec((CONV_K, tn), lambda i, j: (0, j)),
        ],
        out_specs=pl.BlockSpec((tm, tn), lambda i, j: (i, j)),
        out_shape=jax.ShapeDtypeStruct((m, BRANCH), BF16),
        compiler_params=_params("parallel", "arbitrary"),
        name="conv_mixer",
    )(u, u, w_in, w_in, w_in, w_in, conv_w)


def _sb_inproj_kernel(u_ref, w_ref, p_ref, *, q_tiles, z_tile0, scale):
    j = pl.program_id(1)
    r = _mm(u_ref[...], w_ref)

    @pl.when(j < q_tiles)
    def _():
        p_ref[...] = (r * scale).astype(p_ref.dtype)

    @pl.when(jnp.logical_and(j >= q_tiles, j < z_tile0))
    def _():
        p_ref[...] = r.astype(p_ref.dtype)

    @pl.when(j >= z_tile0)
    def _():
        p_ref[...] = _silu(r).astype(p_ref.dtype)


def _sb_inproj(u, w_in, *, tm=1024, tn=512):
    m, d = u.shape
    n = w_in.shape[1]
    chunk_tiles = BRANCH // tn
    return pl.pallas_call(
        functools.partial(_sb_inproj_kernel, q_tiles=chunk_tiles,
                          z_tile0=3 * chunk_tiles,
                          scale=1.0 / math.sqrt(SB_HEAD_DIM)),
        grid=(m // tm, n // tn),
        in_specs=[pl.BlockSpec((tm, d), lambda i, j: (i, 0)),
                  pl.BlockSpec((d, tn), lambda i, j: (0, j))],
        out_specs=pl.BlockSpec((tm, tn), lambda i, j: (i, j)),
        out_shape=jax.ShapeDtypeStruct((m, n), BF16),
        compiler_params=_params("parallel", "arbitrary"),
        name="sb_inproj",
    )(u, w_in)


def _sb_attn_kernel(q_ref, k_ref, v_ref, sz_ref, a_ref, *, blk):
    i = pl.program_id(2)
    q = q_ref[...]
    rows = lax.broadcasted_iota(jnp.int32, (blk, blk), 0)
    cols = lax.broadcasted_iota(jnp.int32, (blk, blk), 1)
    suffix = (rows > cols).astype(BF16)
    causal = cols < rows

    def sweep_block(kb, vb, carry, acc, mask):
        l = lax.dot_general(q, kb, (((1,), (1,)), ((), ())),
                            preferred_element_type=F32)
        t = jnp.log(1.0 + jnp.exp(-jnp.abs(l)))
        log_beta = jnp.minimum(l, 0.0) - t
        log_keep = -jnp.maximum(l, 0.0) - t
        if mask is not None:
            log_keep = jnp.where(mask, log_keep, 0.0)
        tail = jnp.dot(log_keep.astype(BF16), suffix, preferred_element_type=F32)
        w = jnp.exp(log_beta + tail + carry)
        if mask is not None:
            w = jnp.where(mask, w, 0.0)
        acc = acc + jnp.dot(w.astype(BF16), vb, preferred_element_type=F32)
        carry = carry + jnp.sum(log_keep, axis=-1, keepdims=True)
        return carry, acc

    diag = pl.multiple_of(i * blk, blk)
    carry0 = jnp.zeros((blk, 1), F32)
    acc0 = jnp.zeros((blk, SB_HEAD_DIM), F32)
    carry, acc = sweep_block(k_ref[pl.ds(diag, blk), :], v_ref[pl.ds(diag, blk), :],
                             carry0, acc0, causal)

    def body(n, state):
        start = pl.multiple_of((i - 1 - n) * blk, blk)
        return sweep_block(k_ref[pl.ds(start, blk), :], v_ref[pl.ds(start, blk), :],
                           state[0], state[1], None)

    carry, acc = lax.fori_loop(0, i, body, (carry, acc))
    a_ref[...] = (acc * sz_ref[...].astype(F32)).astype(a_ref.dtype)


def _sb_attention(proj, *, batch, seq, blk=256):
    m = proj.shape[0]
    dh = SB_HEAD_DIM
    nq = seq // blk
    return pl.pallas_call(
        functools.partial(_sb_attn_kernel, blk=blk),
        grid=(batch, SB_HEADS, nq),
        in_specs=[
            pl.BlockSpec((blk, dh), lambda b, h, i: (b * nq + i, h)),
            pl.BlockSpec((seq, dh), lambda b, h, i: (b, SB_HEADS + h)),
            pl.BlockSpec((seq, dh), lambda b, h, i: (b, 2 * SB_HEADS + h)),
            pl.BlockSpec((blk, dh), lambda b, h, i: (b * nq + i, 3 * SB_HEADS + h)),
        ],
        out_specs=pl.BlockSpec((blk, dh), lambda b, h, i: (b * nq + i, h)),
        out_shape=jax.ShapeDtypeStruct((m, BRANCH), BF16),
        compiler_params=_params("parallel", "parallel", "arbitrary"),
        name="sb_attention",
    )(proj, proj, proj, proj)


def _outproj_kernel(a_ref, w_ref, h_ref, gpost_ref, gpre_ref, hout_ref, u_ref):
    hn = h_ref[...] + _rms(_mm(a_ref[...], w_ref), gpost_ref[...])
    hout_ref[...] = hn
    u_ref[...] = _rms(hn, gpre_ref[...]).astype(u_ref.dtype)


def _outproj_last_kernel(a_ref, w_ref, h_ref, gpost_ref, hout_ref):
    hout_ref[...] = h_ref[...] + _rms(_mm(a_ref[...], w_ref), gpost_ref[...])


def _outproj(a, w_out, h, g_post, g_pre_next, *, tm=512):
    m, d = h.shape
    row = pl.BlockSpec((tm, d), lambda i: (i, 0))
    gain = pl.BlockSpec((1, d), lambda i: (0, 0))
    w_spec = pl.BlockSpec((BRANCH, d), lambda i: (0, 0))
    if g_pre_next is None:
        return pl.pallas_call(
            _outproj_last_kernel,
            grid=(m // tm,),
            in_specs=[row, w_spec, row, gain],
            out_specs=row,
            out_shape=jax.ShapeDtypeStruct((m, d), F32),
            compiler_params=_params("parallel"),
            name="outproj_last",
        )(a, w_out, h, g_post), None
    return pl.pallas_call(
        _outproj_kernel,
        grid=(m // tm,),
        in_specs=[row, w_spec, row, gain, gain],
        out_specs=[row, row],
        out_shape=[jax.ShapeDtypeStruct((m, d), F32),
                   jax.ShapeDtypeStruct((m, d), BF16)],
        compiler_params=_params("parallel"),
        name="outproj",
    )(a, w_out, h, g_post, g_pre_next)


def kernel(x,
           ln_pre_0, conv_w_in_0, conv_w_0, conv_w_out_0, ln_post_0,
           ln_pre_1, sb_w_in_1, sb_w_out_1, ln_post_1,
           ln_pre_2, conv_w_in_2, conv_w_2, conv_w_out_2, ln_post_2,
           ln_pre_3, sb_w_in_3, sb_w_out_3, ln_post_3):
    batch, seq, d = x.shape
    gain = lambda g: g.reshape(1, d).astype(F32)
    layers = [
        (ln_pre_0, (conv_w_in_0, conv_w_0, conv_w_out_0), ln_post_0),
        (ln_pre_1, (sb_w_in_1, sb_w_out_1), ln_post_1),
        (ln_pre_2, (conv_w_in_2, conv_w_2, conv_w_out_2), ln_post_2),
        (ln_pre_3, (sb_w_in_3, sb_w_out_3), ln_post_3),
    ]
    h = x.reshape(batch * seq, d)
    u = _prenorm(h, gain(layers[0][0]))
    for idx, (_, params, g_post) in enumerate(layers):
        if len(params) == 3:
            w_in, conv_w, w_out = params
            a = _conv_mixer(u, w_in.astype(BF16), conv_w, seq=seq)
        else:
            w_in, w_out = params
            proj = _sb_inproj(u, w_in.astype(BF16))
            a = _sb_attention(proj, batch=batch, seq=seq)
        g_pre_next = gain(layers[idx + 1][0]) if idx + 1 < len(layers) else None
        h, u = _outproj(a, w_out.astype(BF16), h, gain(g_post), g_pre_next)
    return h.reshape(batch, seq, d)
```

```python
import functools
import math

import jax
import jax.numpy as jnp
from jax import lax
from jax.experimental import pallas as pl
from jax.experimental.pallas import tpu as pltpu

D_MODEL = 2048
BRANCH = D_MODEL
CONV_K = 3
SB_HEADS = 16
SB_HEAD_DIM = BRANCH // SB_HEADS
RMS_EPS = 1e-6

BF16_ROW_TILE = 16
VMEM_LIMIT_BYTES = 56 * 1024 * 1024

F32 = jnp.float32
BF16 = jnp.bfloat16
LOG2E = math.log2(math.e)
LOG2_WEIGHT_UNDERFLOW = -152.0


def _params(*semantics):
    return pltpu.CompilerParams(dimension_semantics=semantics,
                                vmem_limit_bytes=VMEM_LIMIT_BYTES)


def _rms(x, g):
    ms = jnp.mean(x * x, axis=-1, keepdims=True)
    return x * lax.rsqrt(ms + RMS_EPS) * g


def _silu(z):
    return z * jax.nn.sigmoid(z)


def _mm(a, w_ref):
    return jnp.dot(a, w_ref[...], preferred_element_type=F32)


def _stage_weights(first_use, pairs):
    @pl.when(first_use)
    def _():
        for w_ref, wb_ref in pairs:
            wb_ref[...] = w_ref[...].astype(wb_ref.dtype)


def _prenorm_kernel(x_ref, g_ref, u_ref):
    u_ref[...] = _rms(x_ref[...], g_ref[...]).astype(u_ref.dtype)


def _prenorm(x, g, *, tm=512):
    m, d = x.shape
    return pl.pallas_call(
        _prenorm_kernel,
        grid=(m // tm,),
        in_specs=[pl.BlockSpec((tm, d), lambda i: (i, 0)),
                  pl.BlockSpec((1, d), lambda i: (0, 0))],
        out_specs=pl.BlockSpec((tm, d), lambda i: (i, 0)),
        out_shape=jax.ShapeDtypeStruct((m, d), BF16),
        compiler_params=_params("parallel"),
        name="prenorm",
    )(x, g)


def _conv_mixer_kernel(u_ref, uh_ref, wb32_ref, wc32_ref, wx32_ref, wz32_ref, cw_ref,
                       a_ref, wb_ref, wc_ref, wx_ref, wz_ref, *, tiles_per_seq):
    i = pl.program_id(1)
    _stage_weights(i == 0, [(wb32_ref, wb_ref), (wc32_ref, wc_ref),
                            (wx32_ref, wx_ref), (wz32_ref, wz_ref)])
    u = u_ref[...]
    cx = _mm(u, wc_ref) * _mm(u, wx_ref)
    uh = uh_ref[...]
    cxh = _mm(uh, wc_ref) * _mm(uh, wx_ref)
    cxh = jnp.where(i % tiles_per_seq == 0, 0.0, cxh)
    full = jnp.concatenate([cxh, cx], axis=0)
    prev1 = pltpu.roll(full, 1, 0)[BF16_ROW_TILE:]
    prev2 = pltpu.roll(full, 2, 0)[BF16_ROW_TILE:]
    cw = cw_ref[...]
    conv = cw[2:3] * cx + cw[1:2] * prev1 + cw[0:1] * prev2
    y = _mm(u, wb_ref) * conv
    z = _mm(u, wz_ref)
    a_ref[...] = (_silu(z) * y).astype(a_ref.dtype)


def _conv_mixer(u, w_in, conv_w, *, seq, tm=1024, tn=256):
    m, d = u.shape
    nj = BRANCH // tn
    halo_blocks_per_tile = tm // BF16_ROW_TILE

    def w_spec(chunk):
        return pl.BlockSpec((d, tn), lambda j, i: (0, chunk * nj + j))

    return pl.pallas_call(
        functools.partial(_conv_mixer_kernel, tiles_per_seq=seq // tm),
        grid=(nj, m // tm),
        in_specs=[
            pl.BlockSpec((tm, d), lambda j, i: (i, 0)),
            pl.BlockSpec((BF16_ROW_TILE, d),
                         lambda j, i: (jnp.maximum(i * halo_blocks_per_tile - 1, 0), 0)),
            w_spec(0), w_spec(1), w_spec(2), w_spec(3),
            pl.BlockSpec((CONV_K, tn), lambda j, i: (0, j)),
        ],
        out_specs=pl.BlockSpec((tm, tn), lambda j, i: (i, j)),
        out_shape=jax.ShapeDtypeStruct((m, BRANCH), BF16),
        scratch_shapes=[pltpu.VMEM((d, tn), BF16)] * 4,
        compiler_params=_params("arbitrary", "arbitrary"),
        name="conv_mixer",
    )(u, u, w_in, w_in, w_in, w_in, conv_w)


def _sb_inproj_kernel(u_ref, w32_ref, p_ref, w_ref, *, q_tiles, z_tile0, scale):
    j = pl.program_id(0)
    _stage_weights(pl.program_id(1) == 0, [(w32_ref, w_ref)])
    r = _mm(u_ref[...], w_ref)

    @pl.when(j < q_tiles)
    def _():
        p_ref[...] = (r * scale).astype(p_ref.dtype)

    @pl.when(jnp.logical_and(j >= q_tiles, j < z_tile0))
    def _():
        p_ref[...] = r.astype(p_ref.dtype)

    @pl.when(j >= z_tile0)
    def _():
        p_ref[...] = _silu(r).astype(p_ref.dtype)


def _sb_inproj(u, w_in, *, tm=1024, tn=1024):
    m, d = u.shape
    n = w_in.shape[1]
    chunk_tiles = BRANCH // tn
    return pl.pallas_call(
        functools.partial(_sb_inproj_kernel, q_tiles=chunk_tiles,
                          z_tile0=3 * chunk_tiles,
                          scale=LOG2E / math.sqrt(SB_HEAD_DIM)),
        grid=(n // tn, m // tm),
        in_specs=[pl.BlockSpec((tm, d), lambda j, i: (i, 0)),
                  pl.BlockSpec((d, tn), lambda j, i: (0, j))],
        out_specs=pl.BlockSpec((tm, tn), lambda j, i: (i, j)),
        out_shape=jax.ShapeDtypeStruct((m, n), BF16),
        scratch_shapes=[pltpu.VMEM((d, tn), BF16)],
        compiler_params=_params("arbitrary", "arbitrary"),
        name="sb_inproj",
    )(u, w_in)


def _sb_attn_kernel(q_ref, k_ref, v_ref, sz_ref, a_ref, acc_ref, carry_ref, *,
                    blk, heads):
    i = pl.program_id(2)
    dh = SB_HEAD_DIM
    lane_reps = blk // dh
    rows = lax.broadcasted_iota(jnp.int32, (blk, blk), 0)
    cols = lax.broadcasted_iota(jnp.int32, (blk, blk), 1)
    suffix = (rows > cols).astype(BF16)
    causal = cols < rows

    def sweep_block(start, mask):
        worst = None
        for g in range(heads):
            lanes = slice(g * dh, (g + 1) * dh)
            l2 = lax.dot_general(q_ref[:, lanes], k_ref[pl.ds(start, blk), lanes],
                                 (((1,), (1,)), ((), ())),
                                 preferred_element_type=F32)
            t2 = jnp.log(1.0 + jnp.exp2(-jnp.abs(l2))) * LOG2E
            log_beta = jnp.minimum(l2, 0.0) - t2
            log_keep = log_beta - l2
            if mask is not None:
                log_keep = jnp.where(mask, log_keep, 0.0)
            tail = jnp.dot(log_keep.astype(BF16), suffix, preferred_element_type=F32)
            carry = carry_ref[g]
            w = jnp.exp2(log_beta + jnp.concatenate([carry] * lane_reps, axis=1) + tail)
            if mask is not None:
                w = jnp.where(mask, w, 0.0)
            acc_ref[g] += jnp.dot(w.astype(BF16), v_ref[pl.ds(start, blk), lanes],
                                  preferred_element_type=F32)
            carry = carry + jnp.sum(log_keep, axis=-1, keepdims=True)
            carry_ref[g] = carry
            worst = carry if worst is None else jnp.maximum(worst, carry)
        return jnp.max(worst)

    acc_ref[...] = jnp.zeros_like(acc_ref)
    carry_ref[...] = jnp.zeros_like(carry_ref)
    sweep_block(pl.multiple_of(i * blk, blk), causal)

    def more_blocks(state):
        n, worst = state
        return jnp.logical_and(n < i, worst > LOG2_WEIGHT_UNDERFLOW)

    def body(state):
        n, _ = state
        return n + 1, sweep_block(pl.multiple_of((i - 1 - n) * blk, blk), None)

    lax.while_loop(more_blocks, body, (jnp.int32(0), jnp.float32(0.0)))
    for g in range(heads):
        lanes = slice(g * dh, (g + 1) * dh)
        a_ref[:, lanes] = (acc_ref[g] * sz_ref[:, lanes].astype(F32)).astype(a_ref.dtype)


def _sb_attention(proj, *, batch, seq, blk=256, heads=4):
    m = proj.shape[0]
    dh = SB_HEAD_DIM
    nq = seq // blk
    groups = SB_HEADS // heads
    width = heads * dh
    return pl.pallas_call(
        functools.partial(_sb_attn_kernel, blk=blk, heads=heads),
        grid=(batch, groups, nq),
        in_specs=[
            pl.BlockSpec((blk, width), lambda b, h, i: (b * nq + i, h)),
            pl.BlockSpec((seq, width), lambda b, h, i: (b, groups + h)),
            pl.BlockSpec((seq, width), lambda b, h, i: (b, 2 * groups + h)),
            pl.BlockSpec((blk, width), lambda b, h, i: (b * nq + i, 3 * groups + h)),
        ],
        out_specs=pl.BlockSpec((blk, width), lambda b, h, i: (b * nq + i, h)),
        out_shape=jax.ShapeDtypeStruct((m, BRANCH), BF16),
        scratch_shapes=[pltpu.VMEM((heads, blk, dh), F32),
                        pltpu.VMEM((heads, blk, dh), F32)],
        compiler_params=_params("parallel", "parallel", "arbitrary"),
        name="sb_attention",
    )(proj, proj, proj, proj)


def _outproj_kernel(a_ref, w32_ref, h_ref, gpost_ref, gpre_ref, hout_ref, u_ref, w_ref):
    _stage_weights(pl.program_id(0) == 0, [(w32_ref, w_ref)])
    hn = h_ref[...] + _rms(_mm(a_ref[...], w_ref), gpost_ref[...])
    hout_ref[...] = hn
    u_ref[...] = _rms(hn, gpre_ref[...]).astype(u_ref.dtype)


def _outproj_last_kernel(a_ref, w32_ref, h_ref, gpost_ref, hout_ref, w_ref):
    _stage_weights(pl.program_id(0) == 0, [(w32_ref, w_ref)])
    hout_ref[...] = h_ref[...] + _rms(_mm(a_ref[...], w_ref), gpost_ref[...])


def _outproj(a, w_out, h, g_post, g_pre_next, *, tm=512):
    m, d = h.shape
    row = pl.BlockSpec((tm, d), lambda i: (i, 0))
    gain = pl.BlockSpec((1, d), lambda i: (0, 0))
    w_spec = pl.BlockSpec((BRANCH, d), lambda i: (0, 0), pipeline_mode=pl.Buffered(1))
    w_scratch = [pltpu.VMEM((BRANCH, d), BF16)]
    if g_pre_next is None:
        return pl.pallas_call(
            _outproj_last_kernel,
            grid=(m // tm,),
            in_specs=[row, w_spec, row, gain],
            out_specs=row,
            out_shape=jax.ShapeDtypeStruct((m, d), F32),
            scratch_shapes=w_scratch,
            compiler_params=_params("arbitrary"),
            name="outproj_last",
        )(a, w_out, h, g_post), None
    return pl.pallas_call(
        _outproj_kernel,
        grid=(m // tm,),
        in_specs=[row, w_spec, row, gain, gain],
        out_specs=[row, row],
        out_shape=[jax.ShapeDtypeStruct((m, d), F32),
                   jax.ShapeDtypeStruct((m, d), BF16)],
        scratch_shapes=w_scratch,
        compiler_params=_params("arbitrary"),
        name="outproj",
    )(a, w_out, h, g_post, g_pre_next)


def kernel(x,
           ln_pre_0, conv_w_in_0, conv_w_0, conv_w_out_0, ln_post_0,
           ln_pre_1, sb_w_in_1, sb_w_out_1, ln_post_1,
           ln_pre_2, conv_w_in_2, conv_w_2, conv_w_out_2, ln_post_2,
           ln_pre_3, sb_w_in_3, sb_w_out_3, ln_post_3):
    batch, seq, d = x.shape
    gain = lambda g: g.reshape(1, d).astype(F32)
    layers = [
        (ln_pre_0, (conv_w_in_0, conv_w_0, conv_w_out_0), ln_post_0),
        (ln_pre_1, (sb_w_in_1, sb_w_out_1), ln_post_1),
        (ln_pre_2, (conv_w_in_2, conv_w_2, conv_w_out_2), ln_post_2),
        (ln_pre_3, (sb_w_in_3, sb_w_out_3), ln_post_3),
    ]
    h = x.reshape(batch * seq, d)
    u = _prenorm(h, gain(layers[0][0]))
    for idx, (_, params, g_post) in enumerate(layers):
        if len(params) == 3:
            w_in, conv_w, w_out = params
            a = _conv_mixer(u, w_in, conv_w, seq=seq)
        else:
            w_in, w_out = params
            proj = _sb_inproj(u, w_in)
            a = _sb_attention(proj, batch=batch, seq=seq)
        g_pre_next = gain(layers[idx + 1][0]) if idx + 1 < len(layers) else None
        h, u = _outproj(a, w_out, h, gain(g_post), g_pre_next)
    return h.reshape(batch, seq, d)
```

```python
import functools
import math

import jax
import jax.numpy as jnp
from jax import lax
from jax.experimental import pallas as pl
from jax.experimental.pallas import tpu as pltpu

D_MODEL = 2048
BRANCH = D_MODEL
CONV_K = 3
SB_HEADS = 16
SB_HEAD_DIM = BRANCH // SB_HEADS
RMS_EPS = 1e-6

BF16_ROW_TILE = 16
VMEM_LIMIT_BYTES = 56 * 1024 * 1024

F32 = jnp.float32
BF16 = jnp.bfloat16
LOG2E = math.log2(math.e)
LOG2_WEIGHT_UNDERFLOW = -152.0


def _params(*semantics, flags=None):
    return pltpu.CompilerParams(dimension_semantics=semantics,
                                vmem_limit_bytes=VMEM_LIMIT_BYTES, flags=flags)


def _rms(x, g):
    ms = jnp.mean(x * x, axis=-1, keepdims=True)
    return x * lax.rsqrt(ms + RMS_EPS) * g


def _silu(z):
    return z * jax.nn.sigmoid(z)


def _mm(a, w_ref):
    return jnp.dot(a, w_ref[...], preferred_element_type=F32)


def _suffix_sum(x, suffix):
    return jnp.dot(x.astype(BF16), suffix, preferred_element_type=F32)


def _stage_weights(first_use, pairs):
    @pl.when(first_use)
    def _():
        for w_ref, wb_ref in pairs:
            wb_ref[...] = w_ref[...].astype(wb_ref.dtype)


def _prenorm_kernel(x_ref, g_ref, u_ref):
    u_ref[...] = _rms(x_ref[...], g_ref[...]).astype(u_ref.dtype)


def _prenorm(x, g, *, tm=512):
    m, d = x.shape
    return pl.pallas_call(
        _prenorm_kernel,
        grid=(m // tm,),
        in_specs=[pl.BlockSpec((tm, d), lambda i: (i, 0)),
                  pl.BlockSpec((1, d), lambda i: (0, 0))],
        out_specs=pl.BlockSpec((tm, d), lambda i: (i, 0)),
        out_shape=jax.ShapeDtypeStruct((m, d), BF16),
        compiler_params=_params("parallel"),
        name="prenorm",
    )(x, g)


def _conv_mixer_kernel(u_ref, uh_ref, wb32_ref, wc32_ref, wx32_ref, wz32_ref, cw_ref,
                       a_ref, wb_ref, wc_ref, wx_ref, wz_ref, *, tiles_per_seq):
    i = pl.program_id(1)
    _stage_weights(i == 0, [(wb32_ref, wb_ref), (wc32_ref, wc_ref),
                            (wx32_ref, wx_ref), (wz32_ref, wz_ref)])
    u = u_ref[...]
    cx = _mm(u, wc_ref) * _mm(u, wx_ref)
    uh = uh_ref[...]
    cxh = _mm(uh, wc_ref) * _mm(uh, wx_ref)
    cxh = jnp.where(i % tiles_per_seq == 0, 0.0, cxh)
    full = jnp.concatenate([cxh, cx], axis=0)
    prev1 = pltpu.roll(full, 1, 0)[BF16_ROW_TILE:]
    prev2 = pltpu.roll(full, 2, 0)[BF16_ROW_TILE:]
    cw = cw_ref[...]
    conv = cw[2:3] * cx + cw[1:2] * prev1 + cw[0:1] * prev2
    y = _mm(u, wb_ref) * conv
    z = _mm(u, wz_ref)
    a_ref[...] = (_silu(z) * y).astype(a_ref.dtype)


def _conv_mixer(u, w_in, conv_w, *, seq, tm=1024, tn=256):
    m, d = u.shape
    nj = BRANCH // tn
    halo_blocks_per_tile = tm // BF16_ROW_TILE

    def w_spec(chunk):
        return pl.BlockSpec((d, tn), lambda j, i: (0, chunk * nj + j))

    return pl.pallas_call(
        functools.partial(_conv_mixer_kernel, tiles_per_seq=seq // tm),
        grid=(nj, m // tm),
        in_specs=[
            pl.BlockSpec((tm, d), lambda j, i: (i, 0)),
            pl.BlockSpec((BF16_ROW_TILE, d),
                         lambda j, i: (jnp.maximum(i * halo_blocks_per_tile - 1, 0), 0)),
            w_spec(0), w_spec(1), w_spec(2), w_spec(3),
            pl.BlockSpec((CONV_K, tn), lambda j, i: (0, j)),
        ],
        out_specs=pl.BlockSpec((tm, tn), lambda j, i: (i, j)),
        out_shape=jax.ShapeDtypeStruct((m, BRANCH), BF16),
        scratch_shapes=[pltpu.VMEM((d, tn), BF16)] * 4,
        compiler_params=_params("arbitrary", "arbitrary"),
        name="conv_mixer",
    )(u, u, w_in, w_in, w_in, w_in, conv_w)


INPROJ_COL_CHUNK = 256


def _sb_inproj_kernel(u_ref, w32_ref, p_ref, w_ref, *, q_tiles, z_tile0, scale):
    j = pl.program_id(0)
    _stage_weights(pl.program_id(1) == 0, [(w32_ref, w_ref)])
    u = u_ref[...]
    col_scale = jnp.where(j < q_tiles, scale, 1.0).astype(F32)
    is_z = j >= z_tile0
    for c in range(p_ref.shape[1] // INPROJ_COL_CHUNK):
        cols = pl.ds(c * INPROJ_COL_CHUNK, INPROJ_COL_CHUNK)
        r = jnp.dot(u, w_ref[:, cols], preferred_element_type=F32)
        p_ref[:, cols] = jnp.where(is_z, _silu(r), r * col_scale).astype(p_ref.dtype)


def _sb_inproj(u, w_in, *, tm=1024, tn=1024):
    m, d = u.shape
    n = w_in.shape[1]
    chunk_tiles = BRANCH // tn
    return pl.pallas_call(
        functools.partial(_sb_inproj_kernel, q_tiles=chunk_tiles,
                          z_tile0=3 * chunk_tiles,
                          scale=LOG2E / math.sqrt(SB_HEAD_DIM)),
        grid=(n // tn, m // tm),
        in_specs=[pl.BlockSpec((tm, d), lambda j, i: (i, 0)),
                  pl.BlockSpec((d, tn), lambda j, i: (0, j))],
        out_specs=pl.BlockSpec((tm, tn), lambda j, i: (i, j)),
        out_shape=jax.ShapeDtypeStruct((m, n), BF16),
        scratch_shapes=[pltpu.VMEM((d, tn), BF16)],
        compiler_params=_params("arbitrary", "arbitrary"),
        name="sb_inproj",
    )(u, w_in)


def _sb_attn_kernel(q_ref, k_ref, v_ref, sz_ref, a_ref, acc_ref, carry_ref, *,
                    blk, heads):
    i = pl.program_id(2)
    dh = SB_HEAD_DIM
    lane_reps = blk // dh
    rows = lax.broadcasted_iota(jnp.int32, (blk, blk), 0)
    cols = lax.broadcasted_iota(jnp.int32, (blk, blk), 1)
    suffix = (rows > cols).astype(BF16)
    causal = cols < rows

    def sweep(blocks, first):
        worst = None
        for g in range(heads):
            lanes = slice(g * dh, (g + 1) * dh)
            q = q_ref[:, lanes]
            carry = jnp.zeros((blk, dh), F32) if first else carry_ref[g]
            acc = jnp.zeros((blk, dh), F32) if first else acc_ref[g]
            for start, mask in blocks:
                l2 = lax.dot_general(q, k_ref[pl.ds(start, blk), lanes],
                                     (((1,), (1,)), ((), ())),
                                     preferred_element_type=F32)
                t2 = jnp.log(1.0 + jnp.exp2(-jnp.abs(l2))) * LOG2E
                log_beta = jnp.minimum(l2, 0.0) - t2
                log_keep = log_beta - l2
                if mask is not None:
                    log_keep = jnp.where(mask, log_keep, 0.0)
                tail = _suffix_sum(log_keep, suffix)
                w = jnp.exp2(log_beta + jnp.concatenate([carry] * lane_reps, axis=1) + tail)
                if mask is not None:
                    w = jnp.where(mask, w, 0.0)
                acc = acc + _mm(w.astype(BF16), v_ref.at[pl.ds(start, blk), lanes])
                carry = carry + jnp.sum(log_keep, axis=-1, keepdims=True)
            acc_ref[g] = acc
            carry_ref[g] = carry
            worst = carry if worst is None else jnp.maximum(worst, carry)
        return jnp.max(worst)

    def key_block(n):
        return pl.multiple_of((i - n) * blk, blk)

    worst = lax.cond(
        i == 0,
        lambda: sweep([(key_block(0), causal)], first=True),
        lambda: sweep([(key_block(0), causal), (key_block(1), None)], first=True))

    def more_blocks(state):
        n, worst = state
        return jnp.logical_and(n <= i, worst > LOG2_WEIGHT_UNDERFLOW)

    def body(state):
        n, _ = state
        return n + 1, sweep([(key_block(n), None)], first=False)

    lax.while_loop(more_blocks, body, (jnp.int32(2), worst))
    for g in range(heads):
        lanes = slice(g * dh, (g + 1) * dh)
        a_ref[:, lanes] = (acc_ref[g] * sz_ref[:, lanes].astype(F32)).astype(a_ref.dtype)


def _sb_attention(proj, *, batch, seq, blk=256, heads=8):
    m = proj.shape[0]
    dh = SB_HEAD_DIM
    nq = seq // blk
    groups = SB_HEADS // heads
    width = heads * dh
    return pl.pallas_call(
        functools.partial(_sb_attn_kernel, blk=blk, heads=heads),
        grid=(batch, groups, nq),
        in_specs=[
            pl.BlockSpec((blk, width), lambda b, h, i: (b * nq + i, h)),
            pl.BlockSpec((seq, width), lambda b, h, i: (b, groups + h)),
            pl.BlockSpec((seq, width), lambda b, h, i: (b, 2 * groups + h)),
            pl.BlockSpec((blk, width), lambda b, h, i: (b * nq + i, 3 * groups + h)),
        ],
        out_specs=pl.BlockSpec((blk, width), lambda b, h, i: (b * nq + i, h)),
        out_shape=jax.ShapeDtypeStruct((m, BRANCH), BF16),
        scratch_shapes=[pltpu.VMEM((heads, blk, dh), F32),
                        pltpu.VMEM((heads, blk, dh), F32)],
        compiler_params=_params("parallel", "parallel", "arbitrary"),
        name="sb_attention",
    )(proj, proj, proj, proj)


OUTPROJ_ROW_CHUNKS = 4


def _outproj_kernel(a_ref, w32_ref, h_ref, gpost_ref, gpre_ref, hout_ref, u_ref, w_ref):
    _stage_weights(pl.program_id(0) == 0, [(w32_ref, w_ref)])
    rows = a_ref.shape[0] // OUTPROJ_ROW_CHUNKS
    for c in range(OUTPROJ_ROW_CHUNKS):
        r = pl.ds(c * rows, rows)
        hn = h_ref[r, :] + _rms(_mm(a_ref[r, :], w_ref), gpost_ref[...])
        hout_ref[r, :] = hn
        u_ref[r, :] = _rms(hn, gpre_ref[...]).astype(u_ref.dtype)


def _outproj_last_kernel(a_ref, w32_ref, h_ref, gpost_ref, hout_ref, w_ref):
    _stage_weights(pl.program_id(0) == 0, [(w32_ref, w_ref)])
    rows = a_ref.shape[0] // OUTPROJ_ROW_CHUNKS
    for c in range(OUTPROJ_ROW_CHUNKS):
        r = pl.ds(c * rows, rows)
        hout_ref[r, :] = h_ref[r, :] + _rms(_mm(a_ref[r, :], w_ref), gpost_ref[...])


def _outproj(a, w_out, h, g_post, g_pre_next, *, tm=512):
    m, d = h.shape
    row = pl.BlockSpec((tm, d), lambda i: (i, 0))
    gain = pl.BlockSpec((1, d), lambda i: (0, 0))
    w_spec = pl.BlockSpec((BRANCH, d), lambda i: (0, 0), pipeline_mode=pl.Buffered(1))
    w_scratch = [pltpu.VMEM((BRANCH, d), BF16)]
    if g_pre_next is None:
        return pl.pallas_call(
            _outproj_last_kernel,
            grid=(m // tm,),
            in_specs=[row, w_spec, row, gain],
            out_specs=row,
            out_shape=jax.ShapeDtypeStruct((m, d), F32),
            scratch_shapes=w_scratch,
            compiler_params=_params("arbitrary"),
            name="outproj_last",
        )(a, w_out, h, g_post), None
    return pl.pallas_call(
        _outproj_kernel,
        grid=(m // tm,),
        in_specs=[row, w_spec, row, gain, gain],
        out_specs=[row, row],
        out_shape=[jax.ShapeDtypeStruct((m, d), F32),
                   jax.ShapeDtypeStruct((m, d), BF16)],
        scratch_shapes=w_scratch,
        compiler_params=_params("arbitrary"),
        name="outproj",
    )(a, w_out, h, g_post, g_pre_next)


def kernel(x,
           ln_pre_0, conv_w_in_0, conv_w_0, conv_w_out_0, ln_post_0,
           ln_pre_1, sb_w_in_1, sb_w_out_1, ln_post_1,
           ln_pre_2, conv_w_in_2, conv_w_2, conv_w_out_2, ln_post_2,
           ln_pre_3, sb_w_in_3, sb_w_out_3, ln_post_3):
    batch, seq, d = x.shape
    gain = lambda g: g.reshape(1, d).astype(F32)
    layers = [
        (ln_pre_0, (conv_w_in_0, conv_w_0, conv_w_out_0), ln_post_0),
        (ln_pre_1, (sb_w_in_1, sb_w_out_1), ln_post_1),
        (ln_pre_2, (conv_w_in_2, conv_w_2, conv_w_out_2), ln_post_2),
        (ln_pre_3, (sb_w_in_3, sb_w_out_3), ln_post_3),
    ]
    h = x.reshape(batch * seq, d)
    u = _prenorm(h, gain(layers[0][0]))
    for idx, (_, params, g_post) in enumerate(layers):
        if len(params) == 3:
            w_in, conv_w, w_out = params
            a = _conv_mixer(u, w_in, conv_w, seq=seq)
        else:
            w_in, w_out = params
            proj = _sb_inproj(u, w_in)
            a = _sb_attention(proj, batch=batch, seq=seq)
        g_pre_next = gain(layers[idx + 1][0]) if idx + 1 < len(layers) else None
        h, u = _outproj(a, w_out, h, gain(g_post), g_pre_next)
    return h.reshape(batch, seq, d)
```

```python
import functools
import math

import jax
import jax.numpy as jnp
from jax import lax
from jax.experimental import pallas as pl
from jax.experimental.pallas import tpu as pltpu

D_MODEL = 2048
BRANCH = D_MODEL
CONV_K = 3
SB_HEADS = 16
SB_HEAD_DIM = BRANCH // SB_HEADS
RMS_EPS = 1e-6

BF16_ROW_TILE = 16
VMEM_LIMIT_BYTES = 56 * 1024 * 1024

F32 = jnp.float32
BF16 = jnp.bfloat16
LOG2E = math.log2(math.e)
LOG2_WEIGHT_UNDERFLOW = -152.0


def _params(*semantics, flags=None):
    return pltpu.CompilerParams(dimension_semantics=semantics,
                                vmem_limit_bytes=VMEM_LIMIT_BYTES, flags=flags)


def _rms(x, g):
    ms = jnp.mean(x * x, axis=-1, keepdims=True)
    return x * lax.rsqrt(ms + RMS_EPS) * g


def _silu(z):
    return z * jax.nn.sigmoid(z)


def _mm(a, w_ref):
    return jnp.dot(a, w_ref[...], preferred_element_type=F32)


def _suffix_sum(x, suffix):
    return jnp.dot(x.astype(BF16), suffix, preferred_element_type=F32)


def _stage_weights(first_use, pairs):
    @pl.when(first_use)
    def _():
        for w_ref, wb_ref in pairs:
            wb_ref[...] = w_ref[...].astype(wb_ref.dtype)


def _prenorm_kernel(x_ref, g_ref, u_ref):
    u_ref[...] = _rms(x_ref[...], g_ref[...]).astype(u_ref.dtype)


def _prenorm(x, g, *, tm=512):
    m, d = x.shape
    return pl.pallas_call(
        _prenorm_kernel,
        grid=(m // tm,),
        in_specs=[pl.BlockSpec((tm, d), lambda i: (i, 0)),
                  pl.BlockSpec((1, d), lambda i: (0, 0))],
        out_specs=pl.BlockSpec((tm, d), lambda i: (i, 0)),
        out_shape=jax.ShapeDtypeStruct((m, d), BF16),
        compiler_params=_params("parallel"),
        name="prenorm",
    )(x, g)


ROW_SUBTILES = 2


def _conv_mixer_kernel(u_ref, uh_ref, wb32_ref, wc32_ref, wx32_ref, wz32_ref, cw_ref,
                       a_ref, wb_ref, wc_ref, wx_ref, wz_ref, *, seq):
    i = pl.program_id(1)
    _stage_weights(i == 0, [(wb32_ref, wb_ref), (wc32_ref, wc_ref),
                            (wx32_ref, wx_ref), (wz32_ref, wz_ref)])
    tm = u_ref.shape[0]
    sub = tm // ROW_SUBTILES
    cw = cw_ref[...]
    for r in range(ROW_SUBTILES):
        u = u_ref[pl.ds(r * sub, sub), :]
        cx = _mm(u, wc_ref) * _mm(u, wx_ref)
        uh = uh_ref[...] if r == 0 else u_ref[pl.ds(r * sub - BF16_ROW_TILE, BF16_ROW_TILE), :]
        cxh = _mm(uh, wc_ref) * _mm(uh, wx_ref)
        cxh = jnp.where((i * tm + r * sub) % seq == 0, 0.0, cxh)
        full = jnp.concatenate([cxh, cx], axis=0)
        prev1 = pltpu.roll(full, 1, 0)[BF16_ROW_TILE:]
        prev2 = pltpu.roll(full, 2, 0)[BF16_ROW_TILE:]
        conv = cw[2:3] * cx + cw[1:2] * prev1 + cw[0:1] * prev2
        y = _mm(u, wb_ref) * conv
        z = _mm(u, wz_ref)
        a_ref[pl.ds(r * sub, sub), :] = (_silu(z) * y).astype(a_ref.dtype)


def _conv_mixer(u, w_in, conv_w, *, seq, tm=2048, tn=256):
    m, d = u.shape
    nj = BRANCH // tn
    halo_blocks_per_tile = tm // BF16_ROW_TILE
    assert seq % (tm // ROW_SUBTILES) == 0, "a sequence must start on a row sub-tile boundary"

    def w_spec(chunk):
        return pl.BlockSpec((d, tn), lambda j, i: (0, chunk * nj + j))

    return pl.pallas_call(
        functools.partial(_conv_mixer_kernel, seq=seq),
        grid=(nj, m // tm),
        in_specs=[
            pl.BlockSpec((tm, d), lambda j, i: (i, 0)),
            pl.BlockSpec((BF16_ROW_TILE, d),
                         lambda j, i: (jnp.maximum(i * halo_blocks_per_tile - 1, 0), 0)),
            w_spec(0), w_spec(1), w_spec(2), w_spec(3),
            pl.BlockSpec((CONV_K, tn), lambda j, i: (0, j)),
        ],
        out_specs=pl.BlockSpec((tm, tn), lambda j, i: (i, j)),
        out_shape=jax.ShapeDtypeStruct((m, BRANCH), BF16),
        scratch_shapes=[pltpu.VMEM((d, tn), BF16)] * 4,
        compiler_params=_params("arbitrary", "arbitrary"),
        name="conv_mixer",
    )(u, u, w_in, w_in, w_in, w_in, conv_w)


INPROJ_COL_CHUNK = 256


def _sb_inproj_kernel(u_ref, w32_ref, p_ref, w_ref, *, q_tiles, z_tile0, scale):
    j = pl.program_id(0)
    _stage_weights(pl.program_id(1) == 0, [(w32_ref, w_ref)])
    sub = u_ref.shape[0] // ROW_SUBTILES
    col_scale = jnp.where(j < q_tiles, scale, 1.0).astype(F32)

    def project(epilogue):
        for r in range(ROW_SUBTILES):
            rows = pl.ds(r * sub, sub)
            u = u_ref[rows, :]
            for c in range(p_ref.shape[1] // INPROJ_COL_CHUNK):
                cols = pl.ds(c * INPROJ_COL_CHUNK, INPROJ_COL_CHUNK)
                acc = jnp.dot(u, w_ref[:, cols], preferred_element_type=F32)
                p_ref[rows, cols] = epilogue(acc).astype(p_ref.dtype)

    @pl.when(j >= z_tile0)
    def _():
        project(_silu)

    @pl.when(j < z_tile0)
    def _():
        project(lambda acc: acc * col_scale)


def _sb_inproj(u, w_in, *, tm=2048, tn=1024):
    m, d = u.shape
    n = w_in.shape[1]
    chunk_tiles = BRANCH // tn
    return pl.pallas_call(
        functools.partial(_sb_inproj_kernel, q_tiles=chunk_tiles,
                          z_tile0=3 * chunk_tiles,
                          scale=LOG2E / math.sqrt(SB_HEAD_DIM)),
        grid=(n // tn, m // tm),
        in_specs=[pl.BlockSpec((tm, d), lambda j, i: (i, 0)),
                  pl.BlockSpec((d, tn), lambda j, i: (0, j))],
        out_specs=pl.BlockSpec((tm, tn), lambda j, i: (i, j)),
        out_shape=jax.ShapeDtypeStruct((m, n), BF16),
        scratch_shapes=[pltpu.VMEM((d, tn), BF16)],
        compiler_params=_params("arbitrary", "arbitrary"),
        name="sb_inproj",
    )(u, w_in)


def _sb_attn_kernel(q_ref, k_ref, v_ref, sz_ref, a_ref, acc_ref, carry_ref, *,
                    blk, heads):
    i = pl.program_id(2)
    dh = SB_HEAD_DIM
    lane_reps = blk // dh
    rows = lax.broadcasted_iota(jnp.int32, (blk, blk), 0)
    cols = lax.broadcasted_iota(jnp.int32, (blk, blk), 1)
    suffix = (rows > cols).astype(BF16)
    causal = cols < rows

    def sweep(blocks, first):
        worst = None
        for g in range(heads):
            lanes = slice(g * dh, (g + 1) * dh)
            q = q_ref[:, lanes]
            carry = jnp.zeros((blk, dh), F32) if first else carry_ref[g]
            acc = jnp.zeros((blk, dh), F32) if first else acc_ref[g]
            for start, mask in blocks:
                l2 = lax.dot_general(q, k_ref[pl.ds(start, blk), lanes],
                                     (((1,), (1,)), ((), ())),
                                     preferred_element_type=F32)
                t2 = jnp.log(1.0 + jnp.exp2(-jnp.abs(l2))) * LOG2E
                log_beta = jnp.minimum(l2, 0.0) - t2
                log_keep = log_beta - l2
                if mask is not None:
                    log_keep = jnp.where(mask, log_keep, 0.0)
                tail = _suffix_sum(log_keep, suffix)
                w = jnp.exp2(log_beta + jnp.concatenate([carry] * lane_reps, axis=1) + tail)
                if mask is not None:
                    w = jnp.where(mask, w, 0.0)
                acc = acc + _mm(w.astype(BF16), v_ref.at[pl.ds(start, blk), lanes])
                carry = carry + jnp.sum(log_keep, axis=-1, keepdims=True)
            acc_ref[g] = acc
            carry_ref[g] = carry
            worst = carry if worst is None else jnp.maximum(worst, carry)
        return jnp.max(worst)

    def key_block(n):
        return pl.multiple_of((i - n) * blk, blk)

    worst = lax.cond(
        i == 0,
        lambda: sweep([(key_block(0), causal)], first=True),
        lambda: sweep([(key_block(0), causal), (key_block(1), None)], first=True))

    def more_blocks(state):
        n, worst = state
        return jnp.logical_and(n <= i, worst > LOG2_WEIGHT_UNDERFLOW)

    def body(state):
        n, _ = state
        return n + 1, sweep([(key_block(n), None)], first=False)

    lax.while_loop(more_blocks, body, (jnp.int32(2), worst))
    for g in range(heads):
        lanes = slice(g * dh, (g + 1) * dh)
        a_ref[:, lanes] = (acc_ref[g] * sz_ref[:, lanes].astype(F32)).astype(a_ref.dtype)


def _sb_attention(proj, *, batch, seq, blk=256, heads=8):
    m = proj.shape[0]
    dh = SB_HEAD_DIM
    nq = seq // blk
    groups = SB_HEADS // heads
    width = heads * dh
    return pl.pallas_call(
        functools.partial(_sb_attn_kernel, blk=blk, heads=heads),
        grid=(batch, groups, nq),
        in_specs=[
            pl.BlockSpec((blk, width), lambda b, h, i: (b * nq + i, h)),
            pl.BlockSpec((seq, width), lambda b, h, i: (b, groups + h)),
            pl.BlockSpec((seq, width), lambda b, h, i: (b, 2 * groups + h)),
            pl.BlockSpec((blk, width), lambda b, h, i: (b * nq + i, 3 * groups + h)),
        ],
        out_specs=pl.BlockSpec((blk, width), lambda b, h, i: (b * nq + i, h)),
        out_shape=jax.ShapeDtypeStruct((m, BRANCH), BF16),
        scratch_shapes=[pltpu.VMEM((heads, blk, dh), F32),
                        pltpu.VMEM((heads, blk, dh), F32)],
        compiler_params=_params("parallel", "parallel", "arbitrary"),
        name="sb_attention",
    )(proj, proj, proj, proj)


OUTPROJ_ROW_CHUNKS = 4


def _outproj_kernel(a_ref, w32_ref, h_ref, gpost_ref, gpre_ref, hout_ref, u_ref, w_ref):
    _stage_weights(pl.program_id(0) == 0, [(w32_ref, w_ref)])
    rows = a_ref.shape[0] // OUTPROJ_ROW_CHUNKS
    for c in range(OUTPROJ_ROW_CHUNKS):
        r = pl.ds(c * rows, rows)
        hn = h_ref[r, :] + _rms(_mm(a_ref[r, :], w_ref), gpost_ref[...])
        hout_ref[r, :] = hn
        u_ref[r, :] = _rms(hn, gpre_ref[...]).astype(u_ref.dtype)


def _outproj_last_kernel(a_ref, w32_ref, h_ref, gpost_ref, hout_ref, w_ref):
    _stage_weights(pl.program_id(0) == 0, [(w32_ref, w_ref)])
    hout_ref[...] = h_ref[...] + _rms(_mm(a_ref[...], w_ref), gpost_ref[...])


def _outproj(a, w_out, h, g_post, g_pre_next, *, tm=512):
    m, d = h.shape
    row = pl.BlockSpec((tm, d), lambda i: (i, 0))
    gain = pl.BlockSpec((1, d), lambda i: (0, 0))
    w_spec = pl.BlockSpec((BRANCH, d), lambda i: (0, 0), pipeline_mode=pl.Buffered(1))
    w_scratch = [pltpu.VMEM((BRANCH, d), BF16)]
    if g_pre_next is None:
        return pl.pallas_call(
            _outproj_last_kernel,
            grid=(m // tm,),
            in_specs=[row, w_spec, row, gain],
            out_specs=row,
            out_shape=jax.ShapeDtypeStruct((m, d), F32),
            scratch_shapes=w_scratch,
            compiler_params=_params("arbitrary"),
            name="outproj_last",
        )(a, w_out, h, g_post), None
    return pl.pallas_call(
        _outproj_kernel,
        grid=(m // tm,),
        in_specs=[row, w_spec, row, gain, gain],
        out_specs=[row, row],
        out_shape=[jax.ShapeDtypeStruct((m, d), F32),
                   jax.ShapeDtypeStruct((m, d), BF16)],
        scratch_shapes=w_scratch,
        compiler_params=_params("arbitrary"),
        name="outproj",
    )(a, w_out, h, g_post, g_pre_next)


def kernel(x,
           ln_pre_0, conv_w_in_0, conv_w_0, conv_w_out_0, ln_post_0,
           ln_pre_1, sb_w_in_1, sb_w_out_1, ln_post_1,
           ln_pre_2, conv_w_in_2, conv_w_2, conv_w_out_2, ln_post_2,
           ln_pre_3, sb_w_in_3, sb_w_out_3, ln_post_3):
    batch, seq, d = x.shape
    gain = lambda g: g.reshape(1, d).astype(F32)
    layers = [
        (ln_pre_0, (conv_w_in_0, conv_w_0, conv_w_out_0), ln_post_0),
        (ln_pre_1, (sb_w_in_1, sb_w_out_1), ln_post_1),
        (ln_pre_2, (conv_w_in_2, conv_w_2, conv_w_out_2), ln_post_2),
        (ln_pre_3, (sb_w_in_3, sb_w_out_3), ln_post_3),
    ]
    h = x.reshape(batch * seq, d)
    u = _prenorm(h, gain(layers[0][0]))
    for idx, (_, params, g_post) in enumerate(layers):
        if len(params) == 3:
            w_in, conv_w, w_out = params
            a = _conv_mixer(u, w_in, conv_w, seq=seq)
        else:
            w_in, w_out = params
            proj = _sb_inproj(u, w_in)
            a = _sb_attention(proj, batch=batch, seq=seq)
        g_pre_next = gain(layers[idx + 1][0]) if idx + 1 < len(layers) else None
        h, u = _outproj(a, w_out, h, gain(g_post), g_pre_next)
    return h.reshape(batch, seq, d)
```

```python
import functools
import math

import jax
import jax.numpy as jnp
from jax import lax
from jax.experimental import pallas as pl
from jax.experimental.pallas import tpu as pltpu

D_MODEL = 2048
BRANCH = D_MODEL
CONV_K = 3
SB_HEADS = 16
SB_HEAD_DIM = BRANCH // SB_HEADS
RMS_EPS = 1e-6

F32_ROW_TILE = 8
VMEM_LIMIT_BYTES = 56 * 1024 * 1024

F32 = jnp.float32
BF16 = jnp.bfloat16
LOG2E = math.log2(math.e)
LOG2_WEIGHT_UNDERFLOW = -152.0


def _params(*semantics, flags=None):
    return pltpu.CompilerParams(dimension_semantics=semantics,
                                vmem_limit_bytes=VMEM_LIMIT_BYTES, flags=flags)


def _rms(x, g):
    ms = jnp.mean(x * x, axis=-1, keepdims=True)
    return x * lax.rsqrt(ms + RMS_EPS) * g


def _silu(z):
    return z * jax.nn.sigmoid(z)


def _mm(a, w_ref):
    return jnp.dot(a, w_ref[...], preferred_element_type=F32)


def _suffix_sum(x, suffix):
    return jnp.dot(x.astype(BF16), suffix, preferred_element_type=F32)


def _stage_weights(first_use, pairs):
    @pl.when(first_use)
    def _():
        for w_ref, wb_ref in pairs:
            wb_ref[...] = w_ref[...].astype(wb_ref.dtype)


def _prenorm_kernel(x_ref, g_ref, u_ref):
    u_ref[...] = _rms(x_ref[...], g_ref[...]).astype(u_ref.dtype)


def _prenorm(x, g, *, tm=512):
    m, d = x.shape
    return pl.pallas_call(
        _prenorm_kernel,
        grid=(m // tm,),
        in_specs=[pl.BlockSpec((tm, d), lambda i: (i, 0)),
                  pl.BlockSpec((1, d), lambda i: (0, 0))],
        out_specs=pl.BlockSpec((tm, d), lambda i: (i, 0)),
        out_shape=jax.ShapeDtypeStruct((m, d), BF16),
        compiler_params=_params("parallel"),
        name="prenorm",
    )(x, g)


ROW_SUBTILES = 2


def _conv_mixer_kernel(u_ref, wb32_ref, wc32_ref, wx32_ref, wz32_ref, cw_ref,
                       a_ref, wb_ref, wc_ref, wx_ref, wz_ref, tail_ref, *, seq):
    i = pl.program_id(1)
    _stage_weights(i == 0, [(wb32_ref, wb_ref), (wc32_ref, wc_ref),
                            (wx32_ref, wx_ref), (wz32_ref, wz_ref)])

    @pl.when(i == 0)
    def _():
        tail_ref[...] = jnp.zeros_like(tail_ref)

    tm = u_ref.shape[0]
    sub = tm // ROW_SUBTILES
    cw = cw_ref[...]
    tail = tail_ref[...]
    for r in range(ROW_SUBTILES):
        u = u_ref[pl.ds(r * sub, sub), :]
        cx = _mm(u, wc_ref) * _mm(u, wx_ref)
        tail = jnp.where((i * tm + r * sub) % seq == 0, 0.0, tail)
        full = jnp.concatenate([tail, cx], axis=0)
        prev1 = pltpu.roll(full, 1, 0)[F32_ROW_TILE:]
        prev2 = pltpu.roll(full, 2, 0)[F32_ROW_TILE:]
        conv = cw[2:3] * cx + cw[1:2] * prev1 + cw[0:1] * prev2
        y = _mm(u, wb_ref) * conv
        z = _mm(u, wz_ref)
        a_ref[pl.ds(r * sub, sub), :] = (_silu(z) * y).astype(a_ref.dtype)
        tail = cx[sub - F32_ROW_TILE:]
    tail_ref[...] = tail


def _conv_mixer(u, w_in, conv_w, *, seq, tm=2048, tn=256):
    m, d = u.shape
    nj = BRANCH // tn
    assert seq % (tm // ROW_SUBTILES) == 0, "a sequence must start on a row sub-tile boundary"

    def w_spec(chunk):
        return pl.BlockSpec((d, tn), lambda j, i: (0, chunk * nj + j))

    return pl.pallas_call(
        functools.partial(_conv_mixer_kernel, seq=seq),
        grid=(nj, m // tm),
        in_specs=[
            pl.BlockSpec((tm, d), lambda j, i: (i, 0)),
            w_spec(0), w_spec(1), w_spec(2), w_spec(3),
            pl.BlockSpec((CONV_K, tn), lambda j, i: (0, j)),
        ],
        out_specs=pl.BlockSpec((tm, tn), lambda j, i: (i, j)),
        out_shape=jax.ShapeDtypeStruct((m, BRANCH), BF16),
        scratch_shapes=[pltpu.VMEM((d, tn), BF16)] * 4 + [pltpu.VMEM((F32_ROW_TILE, tn), F32)],
        compiler_params=_params("arbitrary", "arbitrary"),
        name="conv_mixer",
    )(u, w_in, w_in, w_in, w_in, conv_w)


INPROJ_COL_CHUNK = 256


def _sb_inproj_kernel(u_ref, w32_ref, p_ref, w_ref, *, q_tiles, z_tile0, scale):
    j = pl.program_id(0)
    _stage_weights(pl.program_id(1) == 0, [(w32_ref, w_ref)])
    sub = u_ref.shape[0] // ROW_SUBTILES
    col_scale = jnp.where(j < q_tiles, scale, 1.0).astype(F32)

    def project(epilogue):
        for r in range(ROW_SUBTILES):
            rows = pl.ds(r * sub, sub)
            u = u_ref[rows, :]
            for c in range(p_ref.shape[1] // INPROJ_COL_CHUNK):
                cols = pl.ds(c * INPROJ_COL_CHUNK, INPROJ_COL_CHUNK)
                acc = jnp.dot(u, w_ref[:, cols], preferred_element_type=F32)
                p_ref[rows, cols] = epilogue(acc).astype(p_ref.dtype)

    @pl.when(j >= z_tile0)
    def _():
        project(_silu)

    @pl.when(j < z_tile0)
    def _():
        project(lambda acc: acc * col_scale)


def _sb_inproj(u, w_in, *, tm=2048, tn=1024):
    m, d = u.shape
    n = w_in.shape[1]
    chunk_tiles = BRANCH // tn
    return pl.pallas_call(
        functools.partial(_sb_inproj_kernel, q_tiles=chunk_tiles,
                          z_tile0=3 * chunk_tiles,
                          scale=LOG2E / math.sqrt(SB_HEAD_DIM)),
        grid=(n // tn, m // tm),
        in_specs=[pl.BlockSpec((tm, d), lambda j, i: (i, 0)),
                  pl.BlockSpec((d, tn), lambda j, i: (0, j))],
        out_specs=pl.BlockSpec((tm, tn), lambda j, i: (i, j)),
        out_shape=jax.ShapeDtypeStruct((m, n), BF16),
        scratch_shapes=[pltpu.VMEM((d, tn), BF16)],
        compiler_params=_params("arbitrary", "arbitrary"),
        name="sb_inproj",
    )(u, w_in)


def _sb_attn_kernel(q_ref, k_ref, v_ref, sz_ref, a_ref, acc_ref, carry_ref, *,
                    blk, heads):
    i = pl.program_id(2)
    dh = SB_HEAD_DIM
    lane_reps = blk // dh
    rows = lax.broadcasted_iota(jnp.int32, (blk, blk), 0)
    cols = lax.broadcasted_iota(jnp.int32, (blk, blk), 1)
    suffix = (rows > cols).astype(BF16)
    causal = cols < rows

    def sweep(blocks, first):
        worst = None
        for g in range(heads):
            lanes = slice(g * dh, (g + 1) * dh)
            q = q_ref[:, lanes]
            carry = jnp.zeros((blk, dh), F32) if first else carry_ref[g]
            acc = jnp.zeros((blk, dh), F32) if first else acc_ref[g]
            for start, mask in blocks:
                l2 = lax.dot_general(q, k_ref[pl.ds(start, blk), lanes],
                                     (((1,), (1,)), ((), ())),
                                     preferred_element_type=F32)
                t2 = jnp.log(1.0 + jnp.exp2(-jnp.abs(l2))) * LOG2E
                log_beta = jnp.minimum(l2, 0.0) - t2
                log_keep = log_beta - l2
                if mask is not None:
                    log_keep = jnp.where(mask, log_keep, 0.0)
                tail = _suffix_sum(log_keep, suffix)
                w = jnp.exp2(log_beta + jnp.concatenate([carry] * lane_reps, axis=1) + tail)
                if mask is not None:
                    w = jnp.where(mask, w, 0.0)
                acc = acc + _mm(w.astype(BF16), v_ref.at[pl.ds(start, blk), lanes])
                carry = carry + jnp.sum(log_keep, axis=-1, keepdims=True)
            acc_ref[g] = acc
            carry_ref[g] = carry
            worst = carry if worst is None else jnp.maximum(worst, carry)
        return jnp.max(worst)

    def key_block(n):
        return pl.multiple_of((i - n) * blk, blk)

    worst = lax.cond(
        i == 0,
        lambda: sweep([(key_block(0), causal)], first=True),
        lambda: sweep([(key_block(0), causal), (key_block(1), None)], first=True))

    def more_blocks(state):
        n, worst = state
        return jnp.logical_and(n <= i, worst > LOG2_WEIGHT_UNDERFLOW)

    def body(state):
        n, _ = state
        return n + 1, sweep([(key_block(n), None)], first=False)

    lax.while_loop(more_blocks, body, (jnp.int32(2), worst))
    for g in range(heads):
        lanes = slice(g * dh, (g + 1) * dh)
        a_ref[:, lanes] = (acc_ref[g] * sz_ref[:, lanes].astype(F32)).astype(a_ref.dtype)


def _sb_attention(proj, *, batch, seq, blk=256, heads=8):
    m = proj.shape[0]
    dh = SB_HEAD_DIM
    nq = seq // blk
    groups = SB_HEADS // heads
    width = heads * dh
    return pl.pallas_call(
        functools.partial(_sb_attn_kernel, blk=blk, heads=heads),
        grid=(batch, groups, nq),
        in_specs=[
            pl.BlockSpec((blk, width), lambda b, h, i: (b * nq + i, h)),
            pl.BlockSpec((seq, width), lambda b, h, i: (b, groups + h)),
            pl.BlockSpec((seq, width), lambda b, h, i: (b, 2 * groups + h)),
            pl.BlockSpec((blk, width), lambda b, h, i: (b * nq + i, 3 * groups + h)),
        ],
        out_specs=pl.BlockSpec((blk, width), lambda b, h, i: (b * nq + i, h)),
        out_shape=jax.ShapeDtypeStruct((m, BRANCH), BF16),
        scratch_shapes=[pltpu.VMEM((heads, blk, dh), F32),
                        pltpu.VMEM((heads, blk, dh), F32)],
        compiler_params=_params("parallel", "parallel", "arbitrary"),
        name="sb_attention",
    )(proj, proj, proj, proj)


OUTPROJ_ROW_CHUNKS = 4


def _outproj_kernel(a_ref, w32_ref, h_ref, gpost_ref, gpre_ref, hout_ref, u_ref, w_ref):
    _stage_weights(pl.program_id(0) == 0, [(w32_ref, w_ref)])
    rows = a_ref.shape[0] // OUTPROJ_ROW_CHUNKS
    for c in range(OUTPROJ_ROW_CHUNKS):
        r = pl.ds(c * rows, rows)
        hn = h_ref[r, :] + _rms(_mm(a_ref[r, :], w_ref), gpost_ref[...])
        hout_ref[r, :] = hn
        u_ref[r, :] = _rms(hn, gpre_ref[...]).astype(u_ref.dtype)


def _outproj_last_kernel(a_ref, w32_ref, h_ref, gpost_ref, hout_ref, w_ref):
    _stage_weights(pl.program_id(0) == 0, [(w32_ref, w_ref)])
    hout_ref[...] = h_ref[...] + _rms(_mm(a_ref[...], w_ref), gpost_ref[...])


def _outproj(a, w_out, h, g_post, g_pre_next, *, tm=512):
    m, d = h.shape
    row = pl.BlockSpec((tm, d), lambda i: (i, 0))
    gain = pl.BlockSpec((1, d), lambda i: (0, 0))
    w_spec = pl.BlockSpec((BRANCH, d), lambda i: (0, 0), pipeline_mode=pl.Buffered(1))
    w_scratch = [pltpu.VMEM((BRANCH, d), BF16)]
    if g_pre_next is None:
        return pl.pallas_call(
            _outproj_last_kernel,
            grid=(m // tm,),
            in_specs=[row, w_spec, row, gain],
            out_specs=row,
            out_shape=jax.ShapeDtypeStruct((m, d), F32),
            scratch_shapes=w_scratch,
            compiler_params=_params("arbitrary"),
            name="outproj_last",
        )(a, w_out, h, g_post), None
    return pl.pallas_call(
        _outproj_kernel,
        grid=(m // tm,),
        in_specs=[row, w_spec, row, gain, gain],
        out_specs=[row, row],
        out_shape=[jax.ShapeDtypeStruct((m, d), F32),
                   jax.ShapeDtypeStruct((m, d), BF16)],
        scratch_shapes=w_scratch,
        compiler_params=_params("arbitrary"),
        name="outproj",
    )(a, w_out, h, g_post, g_pre_next)


def kernel(x,
           ln_pre_0, conv_w_in_0, conv_w_0, conv_w_out_0, ln_post_0,
           ln_pre_1, sb_w_in_1, sb_w_out_1, ln_post_1,
           ln_pre_2, conv_w_in_2, conv_w_2, conv_w_out_2, ln_post_2,
           ln_pre_3, sb_w_in_3, sb_w_out_3, ln_post_3):
    batch, seq, d = x.shape
    gain = lambda g: g.reshape(1, d).astype(F32)
    layers = [
        (ln_pre_0, (conv_w_in_0, conv_w_0, conv_w_out_0), ln_post_0),
        (ln_pre_1, (sb_w_in_1, sb_w_out_1), ln_post_1),
        (ln_pre_2, (conv_w_in_2, conv_w_2, conv_w_out_2), ln_post_2),
        (ln_pre_3, (sb_w_in_3, sb_w_out_3), ln_post_3),
    ]
    h = x.reshape(batch * seq, d)
    u = _prenorm(h, gain(layers[0][0]))
    for idx, (_, params, g_post) in enumerate(layers):
        if len(params) == 3:
            w_in, conv_w, w_out = params
            a = _conv_mixer(u, w_in, conv_w, seq=seq)
        else:
            w_in, w_out = params
            proj = _sb_inproj(u, w_in)
            a = _sb_attention(proj, batch=batch, seq=seq)
        g_pre_next = gain(layers[idx + 1][0]) if idx + 1 < len(layers) else None
        h, u = _outproj(a, w_out, h, gain(g_post), g_pre_next)
    return h.reshape(batch, seq, d)
```

```python
import functools
import math

import jax
import jax.numpy as jnp
from jax import lax
from jax.experimental import pallas as pl
from jax.experimental.pallas import tpu as pltpu

D_MODEL = 2048
BRANCH = D_MODEL
CONV_K = 3
SB_HEADS = 16
SB_HEAD_DIM = BRANCH // SB_HEADS
RMS_EPS = 1e-6

F32_ROW_TILE = 8
VMEM_LIMIT_BYTES = 56 * 1024 * 1024

F32 = jnp.float32
BF16 = jnp.bfloat16
LOG2E = math.log2(math.e)
LOG2_WEIGHT_UNDERFLOW = -152.0


def _params(*semantics, flags=None):
    return pltpu.CompilerParams(dimension_semantics=semantics,
                                vmem_limit_bytes=VMEM_LIMIT_BYTES, flags=flags)


def _rms(x, g):
    ms = jnp.mean(x * x, axis=-1, keepdims=True)
    return x * lax.rsqrt(ms + RMS_EPS) * g


def _silu(z):
    return z * jax.nn.sigmoid(z)


def _mm(a, w_ref):
    return jnp.dot(a, w_ref[...], preferred_element_type=F32)


def _suffix_sum(x, suffix):
    return jnp.dot(x.astype(BF16), suffix, preferred_element_type=F32)


def _stage_weights(first_use, pairs):
    @pl.when(first_use)
    def _():
        for w_ref, wb_ref in pairs:
            wb_ref[...] = w_ref[...].astype(wb_ref.dtype)


def _prenorm_kernel(x_ref, g_ref, u_ref):
    u_ref[...] = _rms(x_ref[...], g_ref[...]).astype(u_ref.dtype)


def _prenorm(x, g, *, tm=512):
    m, d = x.shape
    return pl.pallas_call(
        _prenorm_kernel,
        grid=(m // tm,),
        in_specs=[pl.BlockSpec((tm, d), lambda i: (i, 0)),
                  pl.BlockSpec((1, d), lambda i: (0, 0))],
        out_specs=pl.BlockSpec((tm, d), lambda i: (i, 0)),
        out_shape=jax.ShapeDtypeStruct((m, d), BF16),
        compiler_params=_params("parallel"),
        name="prenorm",
    )(x, g)


ROW_SUBTILES = 2


def _conv_mixer_kernel(u_ref, wb32_ref, wc32_ref, wx32_ref, wz32_ref, cw_ref,
                       a_ref, wb_ref, wc_ref, wx_ref, wz_ref, tail_ref, *, seq):
    i = pl.program_id(1)
    _stage_weights(i == 0, [(wb32_ref, wb_ref), (wc32_ref, wc_ref),
                            (wx32_ref, wx_ref), (wz32_ref, wz_ref)])

    @pl.when(i == 0)
    def _():
        tail_ref[...] = jnp.zeros_like(tail_ref)

    tm = u_ref.shape[0]
    sub = tm // ROW_SUBTILES
    cw = cw_ref[...]
    tail = tail_ref[...]
    for r in range(ROW_SUBTILES):
        u = u_ref[pl.ds(r * sub, sub), :]
        cx = _mm(u, wc_ref) * _mm(u, wx_ref)
        tail = jnp.where((i * tm + r * sub) % seq == 0, 0.0, tail)
        full = jnp.concatenate([tail, cx], axis=0)
        prev1 = pltpu.roll(full, 1, 0)[F32_ROW_TILE:]
        prev2 = pltpu.roll(full, 2, 0)[F32_ROW_TILE:]
        conv = cw[2:3] * cx + cw[1:2] * prev1 + cw[0:1] * prev2
        y = _mm(u, wb_ref) * conv
        z = _mm(u, wz_ref)
        a_ref[pl.ds(r * sub, sub), :] = (_silu(z) * y).astype(a_ref.dtype)
        tail = cx[sub - F32_ROW_TILE:]
    tail_ref[...] = tail


def _conv_mixer(u, w_in, conv_w, *, seq, tm=2048, tn=256):
    m, d = u.shape
    nj = BRANCH // tn
    assert seq % (tm // ROW_SUBTILES) == 0, "a sequence must start on a row sub-tile boundary"

    def w_spec(chunk):
        return pl.BlockSpec((d, tn), lambda j, i: (0, chunk * nj + j))

    return pl.pallas_call(
        functools.partial(_conv_mixer_kernel, seq=seq),
        grid=(nj, m // tm),
        in_specs=[
            pl.BlockSpec((tm, d), lambda j, i: (i, 0)),
            w_spec(0), w_spec(1), w_spec(2), w_spec(3),
            pl.BlockSpec((CONV_K, tn), lambda j, i: (0, j)),
        ],
        out_specs=pl.BlockSpec((tm, tn), lambda j, i: (i, j)),
        out_shape=jax.ShapeDtypeStruct((m, BRANCH), BF16),
        scratch_shapes=[pltpu.VMEM((d, tn), BF16)] * 4 + [pltpu.VMEM((F32_ROW_TILE, tn), F32)],
        compiler_params=_params("arbitrary", "arbitrary"),
        name="conv_mixer",
    )(u, w_in, w_in, w_in, w_in, conv_w)


INPROJ_COL_CHUNK = 256


def _sb_inproj_kernel(u_ref, w32_ref, p_ref, w_ref, *, q_tiles, z_tile0, scale):
    j = pl.program_id(0)
    _stage_weights(pl.program_id(1) == 0, [(w32_ref, w_ref)])
    sub = u_ref.shape[0] // ROW_SUBTILES
    col_scale = jnp.where(j < q_tiles, scale, 1.0).astype(F32)

    def project(epilogue):
        for r in range(ROW_SUBTILES):
            rows = pl.ds(r * sub, sub)
            u = u_ref[rows, :]
            for c in range(p_ref.shape[1] // INPROJ_COL_CHUNK):
                cols = pl.ds(c * INPROJ_COL_CHUNK, INPROJ_COL_CHUNK)
                acc = jnp.dot(u, w_ref[:, cols], preferred_element_type=F32)
                p_ref[rows, cols] = epilogue(acc).astype(p_ref.dtype)

    @pl.when(j >= z_tile0)
    def _():
        project(_silu)

    @pl.when(j < z_tile0)
    def _():
        project(lambda acc: acc * col_scale)


def _sb_inproj(u, w_in, *, tm=2048, tn=1024):
    m, d = u.shape
    n = w_in.shape[1]
    chunk_tiles = BRANCH // tn
    return pl.pallas_call(
        functools.partial(_sb_inproj_kernel, q_tiles=chunk_tiles,
                          z_tile0=3 * chunk_tiles,
                          scale=LOG2E / math.sqrt(SB_HEAD_DIM)),
        grid=(n // tn, m // tm),
        in_specs=[pl.BlockSpec((tm, d), lambda j, i: (i, 0)),
                  pl.BlockSpec((d, tn), lambda j, i: (0, j))],
        out_specs=pl.BlockSpec((tm, tn), lambda j, i: (i, j)),
        out_shape=jax.ShapeDtypeStruct((m, n), BF16),
        scratch_shapes=[pltpu.VMEM((d, tn), BF16)],
        compiler_params=_params("arbitrary", "arbitrary"),
        name="sb_inproj",
    )(u, w_in)


ATTN_QUERY_BLOCKS = 2


def _sb_attn_kernel(q_ref, k_ref, v_ref, sz_ref, a_ref, acc_ref, carry_ref, *,
                    blk, heads):
    tile = pl.program_id(2)
    dh = SB_HEAD_DIM
    lane_reps = blk // dh
    rows = lax.broadcasted_iota(jnp.int32, (blk, blk), 0)
    cols = lax.broadcasted_iota(jnp.int32, (blk, blk), 1)
    suffix = (rows > cols).astype(BF16)
    causal = cols < rows

    def query_block(slot):
        return tile * ATTN_QUERY_BLOCKS + slot

    def key_start(slot, n):
        return pl.multiple_of((query_block(slot) - n) * blk, blk)

    def sweep(work, first):
        worsts = []
        for slot, blocks in work:
            q_rows = slice(slot * blk, (slot + 1) * blk)
            worst = None
            for g in range(heads):
                lanes = slice(g * dh, (g + 1) * dh)
                q = q_ref[q_rows, lanes]
                carry = jnp.zeros((blk, dh), F32) if first else carry_ref[slot, g]
                acc = jnp.zeros((blk, dh), F32) if first else acc_ref[slot, g]
                for start, mask in blocks:
                    l2 = lax.dot_general(q, k_ref[pl.ds(start, blk), lanes],
                                         (((1,), (1,)), ((), ())),
                                         preferred_element_type=F32)
                    t2 = jnp.log(1.0 + jnp.exp2(-jnp.abs(l2))) * LOG2E
                    log_beta = jnp.minimum(l2, 0.0) - t2
                    log_keep = log_beta - l2
                    if mask is not None:
                        log_keep = jnp.where(mask, log_keep, 0.0)
                    tail = _suffix_sum(log_keep, suffix)
                    w = jnp.exp2(log_beta + jnp.concatenate([carry] * lane_reps, axis=1) + tail)
                    if mask is not None:
                        w = jnp.where(mask, w, 0.0)
                    acc = acc + _mm(w.astype(BF16), v_ref.at[pl.ds(start, blk), lanes])
                    carry = carry + jnp.sum(log_keep, axis=-1, keepdims=True)
                acc_ref[slot, g] = acc
                carry_ref[slot, g] = carry
                worst = carry if worst is None else jnp.maximum(worst, carry)
            worsts.append(jnp.max(worst))
        return tuple(worsts)

    slots = range(ATTN_QUERY_BLOCKS)

    def first_sweep(first_has_left):
        work = []
        for slot in slots:
            blocks = [(key_start(slot, 0), causal)]
            if slot > 0 or first_has_left:
                blocks.append((key_start(slot, 1), None))
            work.append((slot, blocks))
        return sweep(work, first=True)

    worsts = lax.cond(tile == 0, lambda: first_sweep(False), lambda: first_sweep(True))

    def wants(slot, n, worst):
        return jnp.logical_and(n <= query_block(slot), worst > LOG2_WEIGHT_UNDERFLOW)

    def more_blocks(state):
        n, worsts = state
        more = wants(0, n, worsts[0])
        for slot in slots[1:]:
            more = jnp.logical_or(more, wants(slot, n, worsts[slot]))
        return more

    def body(state):
        n, worsts = state
        new = []
        for slot in slots:
            new.append(lax.cond(
                wants(slot, n, worsts[slot]),
                lambda slot=slot: sweep([(slot, [(key_start(slot, n), None)])], first=False)[0],
                lambda slot=slot: worsts[slot]))
        return n + 1, tuple(new)

    lax.while_loop(more_blocks, body, (jnp.int32(2), worsts))
    for slot in slots:
        q_rows = slice(slot * blk, (slot + 1) * blk)
        for g in range(heads):
            lanes = slice(g * dh, (g + 1) * dh)
            a_ref[q_rows, lanes] = (acc_ref[slot, g]
                                    * sz_ref[q_rows, lanes].astype(F32)).astype(a_ref.dtype)


def _sb_attention(proj, *, batch, seq, blk=256, heads=8):
    m = proj.shape[0]
    dh = SB_HEAD_DIM
    tile_rows = blk * ATTN_QUERY_BLOCKS
    nq = seq // tile_rows
    groups = SB_HEADS // heads
    width = heads * dh
    return pl.pallas_call(
        functools.partial(_sb_attn_kernel, blk=blk, heads=heads),
        grid=(batch, groups, nq),
        in_specs=[
            pl.BlockSpec((tile_rows, width), lambda b, h, i: (b * nq + i, h)),
            pl.BlockSpec((seq, width), lambda b, h, i: (b, groups + h)),
            pl.BlockSpec((seq, width), lambda b, h, i: (b, 2 * groups + h)),
            pl.BlockSpec((tile_rows, width), lambda b, h, i: (b * nq + i, 3 * groups + h)),
        ],
        out_specs=pl.BlockSpec((tile_rows, width), lambda b, h, i: (b * nq + i, h)),
        out_shape=jax.ShapeDtypeStruct((m, BRANCH), BF16),
        scratch_shapes=[pltpu.VMEM((ATTN_QUERY_BLOCKS, heads, blk, dh), F32),
                        pltpu.VMEM((ATTN_QUERY_BLOCKS, heads, blk, dh), F32)],
        compiler_params=_params("parallel", "parallel", "arbitrary"),
        name="sb_attention",
    )(proj, proj, proj, proj)


OUTPROJ_ROW_CHUNKS = 4


def _outproj_kernel(a_ref, w32_ref, h_ref, gpost_ref, gpre_ref, hout_ref, u_ref, w_ref):
    _stage_weights(pl.program_id(0) == 0, [(w32_ref, w_ref)])
    rows = a_ref.shape[0] // OUTPROJ_ROW_CHUNKS
    for c in range(OUTPROJ_ROW_CHUNKS):
        r = pl.ds(c * rows, rows)
        hn = h_ref[r, :] + _rms(_mm(a_ref[r, :], w_ref), gpost_ref[...])
        hout_ref[r, :] = hn
        u_ref[r, :] = _rms(hn, gpre_ref[...]).astype(u_ref.dtype)


def _outproj_last_kernel(a_ref, w32_ref, h_ref, gpost_ref, hout_ref, w_ref):
    _stage_weights(pl.program_id(0) == 0, [(w32_ref, w_ref)])
    hout_ref[...] = h_ref[...] + _rms(_mm(a_ref[...], w_ref), gpost_ref[...])


def _outproj(a, w_out, h, g_post, g_pre_next, *, tm=512):
    m, d = h.shape
    row = pl.BlockSpec((tm, d), lambda i: (i, 0))
    gain = pl.BlockSpec((1, d), lambda i: (0, 0))
    w_spec = pl.BlockSpec((BRANCH, d), lambda i: (0, 0), pipeline_mode=pl.Buffered(1))
    w_scratch = [pltpu.VMEM((BRANCH, d), BF16)]
    if g_pre_next is None:
        return pl.pallas_call(
            _outproj_last_kernel,
            grid=(m // tm,),
            in_specs=[row, w_spec, row, gain],
            out_specs=row,
            out_shape=jax.ShapeDtypeStruct((m, d), F32),
            scratch_shapes=w_scratch,
            compiler_params=_params("arbitrary"),
            name="outproj_last",
        )(a, w_out, h, g_post), None
    return pl.pallas_call(
        _outproj_kernel,
        grid=(m // tm,),
        in_specs=[row, w_spec, row, gain, gain],
        out_specs=[row, row],
        out_shape=[jax.ShapeDtypeStruct((m, d), F32),
                   jax.ShapeDtypeStruct((m, d), BF16)],
        scratch_shapes=w_scratch,
        compiler_params=_params("arbitrary"),
        name="outproj",
    )(a, w_out, h, g_post, g_pre_next)


def kernel(x,
           ln_pre_0, conv_w_in_0, conv_w_0, conv_w_out_0, ln_post_0,
           ln_pre_1, sb_w_in_1, sb_w_out_1, ln_post_1,
           ln_pre_2, conv_w_in_2, conv_w_2, conv_w_out_2, ln_post_2,
           ln_pre_3, sb_w_in_3, sb_w_out_3, ln_post_3):
    batch, seq, d = x.shape
    gain = lambda g: g.reshape(1, d).astype(F32)
    layers = [
        (ln_pre_0, (conv_w_in_0, conv_w_0, conv_w_out_0), ln_post_0),
        (ln_pre_1, (sb_w_in_1, sb_w_out_1), ln_post_1),
        (ln_pre_2, (conv_w_in_2, conv_w_2, conv_w_out_2), ln_post_2),
        (ln_pre_3, (sb_w_in_3, sb_w_out_3), ln_post_3),
    ]
    h = x.reshape(batch * seq, d)
    u = _prenorm(h, gain(layers[0][0]))
    for idx, (_, params, g_post) in enumerate(layers):
        if len(params) == 3:
            w_in, conv_w, w_out = params
            a = _conv_mixer(u, w_in, conv_w, seq=seq)
        else:
            w_in, w_out = params
            proj = _sb_inproj(u, w_in)
            a = _sb_attention(proj, batch=batch, seq=seq)
        g_pre_next = gain(layers[idx + 1][0]) if idx + 1 < len(layers) else None
        h, u = _outproj(a, w_out, h, gain(g_post), g_pre_next)
    return h.reshape(batch, seq, d)
```

```python
import functools
import math

import jax
import jax.numpy as jnp
from jax import lax
from jax.experimental import pallas as pl
from jax.experimental.pallas import tpu as pltpu

D_MODEL = 2048
BRANCH = D_MODEL
CONV_K = 3
SB_HEADS = 16
SB_HEAD_DIM = BRANCH // SB_HEADS
RMS_EPS = 1e-6

F32_ROW_TILE = 8
VMEM_LIMIT_BYTES = 56 * 1024 * 1024

F32 = jnp.float32
BF16 = jnp.bfloat16
LOG2E = math.log2(math.e)
LOG2_WEIGHT_UNDERFLOW = -152.0


def _params(*semantics):
    return pltpu.CompilerParams(dimension_semantics=semantics,
                                vmem_limit_bytes=VMEM_LIMIT_BYTES)


def _rms(x, g):
    ms = jnp.mean(x * x, axis=-1, keepdims=True)
    return x * lax.rsqrt(ms + RMS_EPS) * g


def _silu(z):
    return z * jax.nn.sigmoid(z)


def _mm(a, w_ref):
    return jnp.dot(a, w_ref[...], preferred_element_type=F32)


def _suffix_sum(x, suffix):
    return jnp.dot(x.astype(BF16), suffix, preferred_element_type=F32)


def _stage_weights(first_use, pairs):
    @pl.when(first_use)
    def _():
        for w_ref, wb_ref in pairs:
            wb_ref[...] = w_ref[...].astype(wb_ref.dtype)


def _prenorm_kernel(x_ref, g_ref, u_ref):
    u_ref[...] = _rms(x_ref[...], g_ref[...]).astype(u_ref.dtype)


def _prenorm(x, g, *, tm=512):
    m, d = x.shape
    return pl.pallas_call(
        _prenorm_kernel,
        grid=(m // tm,),
        in_specs=[pl.BlockSpec((tm, d), lambda i: (i, 0)),
                  pl.BlockSpec((1, d), lambda i: (0, 0))],
        out_specs=pl.BlockSpec((tm, d), lambda i: (i, 0)),
        out_shape=jax.ShapeDtypeStruct((m, d), BF16),
        compiler_params=_params("parallel"),
        name="prenorm",
    )(x, g)


ROW_SUBTILES = 2


def _conv_mixer_kernel(u_ref, wb32_ref, wc32_ref, wx32_ref, wz32_ref, cw_ref,
                       a_ref, wb_ref, wc_ref, wx_ref, wz_ref, tail_ref, *, seq):
    i = pl.program_id(1)
    _stage_weights(i == 0, [(wb32_ref, wb_ref), (wc32_ref, wc_ref),
                            (wx32_ref, wx_ref), (wz32_ref, wz_ref)])

    @pl.when(i == 0)
    def _():
        tail_ref[...] = jnp.zeros_like(tail_ref)

    tm = u_ref.shape[0]
    sub = tm // ROW_SUBTILES
    cw = cw_ref[...]
    tail = tail_ref[...]
    for r in range(ROW_SUBTILES):
        u = u_ref[pl.ds(r * sub, sub), :]
        cx = _mm(u, wc_ref) * _mm(u, wx_ref)
        tail = jnp.where((i * tm + r * sub) % seq == 0, 0.0, tail)
        full = jnp.concatenate([tail, cx], axis=0)
        prev1 = pltpu.roll(full, 1, 0)[F32_ROW_TILE:]
        prev2 = pltpu.roll(full, 2, 0)[F32_ROW_TILE:]
        conv = cw[2:3] * cx + cw[1:2] * prev1 + cw[0:1] * prev2
        y = _mm(u, wb_ref) * conv
        z = _mm(u, wz_ref)
        a_ref[pl.ds(r * sub, sub), :] = (_silu(z) * y).astype(a_ref.dtype)
        tail = cx[sub - F32_ROW_TILE:]
    tail_ref[...] = tail


def _conv_mixer(u, w_in, conv_w, *, seq, tm=2048, tn=256):
    m, d = u.shape
    nj = BRANCH // tn
    assert seq % (tm // ROW_SUBTILES) == 0, "a sequence must start on a row sub-tile boundary"

    def w_spec(chunk):
        return pl.BlockSpec((d, tn), lambda j, i: (0, chunk * nj + j))

    return pl.pallas_call(
        functools.partial(_conv_mixer_kernel, seq=seq),
        grid=(nj, m // tm),
        in_specs=[
            pl.BlockSpec((tm, d), lambda j, i: (i, 0)),
            w_spec(0), w_spec(1), w_spec(2), w_spec(3),
            pl.BlockSpec((CONV_K, tn), lambda j, i: (0, j)),
        ],
        out_specs=pl.BlockSpec((tm, tn), lambda j, i: (i, j)),
        out_shape=jax.ShapeDtypeStruct((m, BRANCH), BF16),
        scratch_shapes=[pltpu.VMEM((d, tn), BF16)] * 4 + [pltpu.VMEM((F32_ROW_TILE, tn), F32)],
        compiler_params=_params("arbitrary", "arbitrary"),
        name="conv_mixer",
    )(u, w_in, w_in, w_in, w_in, conv_w)


INPROJ_COL_CHUNK = 256


def _sb_inproj_kernel(u_ref, w32_ref, p_ref, w_ref, *, q_tiles, z_tile0, scale):
    j = pl.program_id(0)
    _stage_weights(pl.program_id(1) == 0, [(w32_ref, w_ref)])
    sub = u_ref.shape[0] // ROW_SUBTILES
    col_scale = jnp.where(j < q_tiles, scale, 1.0).astype(F32)

    def project(epilogue):
        for r in range(ROW_SUBTILES):
            rows = pl.ds(r * sub, sub)
            u = u_ref[rows, :]
            for c in range(p_ref.shape[1] // INPROJ_COL_CHUNK):
                cols = pl.ds(c * INPROJ_COL_CHUNK, INPROJ_COL_CHUNK)
                acc = jnp.dot(u, w_ref[:, cols], preferred_element_type=F32)
                p_ref[rows, cols] = epilogue(acc).astype(p_ref.dtype)

    @pl.when(j >= z_tile0)
    def _():
        project(_silu)

    @pl.when(j < z_tile0)
    def _():
        project(lambda acc: acc * col_scale)


def _sb_inproj(u, w_in, *, tm=2048, tn=1024):
    m, d = u.shape
    n = w_in.shape[1]
    chunk_tiles = BRANCH // tn
    return pl.pallas_call(
        functools.partial(_sb_inproj_kernel, q_tiles=chunk_tiles,
                          z_tile0=3 * chunk_tiles,
                          scale=LOG2E / math.sqrt(SB_HEAD_DIM)),
        grid=(n // tn, m // tm),
        in_specs=[pl.BlockSpec((tm, d), lambda j, i: (i, 0)),
                  pl.BlockSpec((d, tn), lambda j, i: (0, j))],
        out_specs=pl.BlockSpec((tm, tn), lambda j, i: (i, j)),
        out_shape=jax.ShapeDtypeStruct((m, n), BF16),
        scratch_shapes=[pltpu.VMEM((d, tn), BF16)],
        compiler_params=_params("arbitrary", "arbitrary"),
        name="sb_inproj",
    )(u, w_in)


ATTN_BLOCK = 256
ATTN_HEADS_PER_STEP = 8
ATTN_QUERY_BLOCKS = 2


def _sb_attn_kernel(q_ref, k_ref, v_ref, sz_ref, a_ref, acc_ref, carry_ref, *,
                    blk, heads):
    tile = pl.program_id(2)
    dh = SB_HEAD_DIM
    lane_reps = blk // dh
    rows = lax.broadcasted_iota(jnp.int32, (blk, blk), 0)
    cols = lax.broadcasted_iota(jnp.int32, (blk, blk), 1)
    suffix = (rows > cols).astype(BF16)
    causal = cols < rows

    def query_block(slot):
        return tile * ATTN_QUERY_BLOCKS + slot

    def key_start(slot, n):
        return pl.multiple_of((query_block(slot) - n) * blk, blk)

    def sweep(work, first):
        worsts = []
        for slot, blocks in work:
            q_rows = slice(slot * blk, (slot + 1) * blk)
            worst = None
            for g in range(heads):
                lanes = slice(g * dh, (g + 1) * dh)
                q = q_ref[q_rows, lanes]
                carry = jnp.zeros((blk, dh), F32) if first else carry_ref[slot, g]
                acc = jnp.zeros((blk, dh), F32) if first else acc_ref[slot, g]
                gates = []
                for start, mask in blocks:
                    l2 = lax.dot_general(q, k_ref[pl.ds(start, blk), lanes],
                                         (((1,), (1,)), ((), ())),
                                         preferred_element_type=F32)
                    t2 = jnp.log(1.0 + jnp.exp2(-jnp.abs(l2))) * LOG2E
                    log_beta = jnp.minimum(l2, 0.0) - t2
                    log_keep = log_beta - l2
                    if mask is not None:
                        log_keep = jnp.where(mask, log_keep, 0.0)
                    gates.append((log_beta, log_keep))
                tails = _suffix_sum(jnp.concatenate([keep for _, keep in gates], axis=0), suffix)
                for idx, (start, mask) in enumerate(blocks):
                    log_beta, log_keep = gates[idx]
                    tail = tails[idx * blk:(idx + 1) * blk]
                    w = jnp.exp2(log_beta + jnp.concatenate([carry] * lane_reps, axis=1) + tail)
                    if mask is not None:
                        w = jnp.where(mask, w, 0.0)
                    acc = acc + _mm(w.astype(BF16), v_ref.at[pl.ds(start, blk), lanes])
                    carry = carry + jnp.sum(log_keep, axis=-1, keepdims=True)
                acc_ref[slot, g] = acc
                carry_ref[slot, g] = carry
                worst = carry if worst is None else jnp.maximum(worst, carry)
            worsts.append(jnp.max(worst))
        return tuple(worsts)

    slots = range(ATTN_QUERY_BLOCKS)

    def first_sweep(first_has_left):
        work = []
        for slot in slots:
            blocks = [(key_start(slot, 0), causal)]
            if slot > 0 or first_has_left:
                blocks.append((key_start(slot, 1), None))
            work.append((slot, blocks))
        return sweep(work, first=True)

    worsts = lax.cond(tile == 0, lambda: first_sweep(False), lambda: first_sweep(True))

    def wants(slot, n, worst):
        return jnp.logical_and(n <= query_block(slot), worst > LOG2_WEIGHT_UNDERFLOW)

    def more_blocks(state):
        n, worsts = state
        more = wants(0, n, worsts[0])
        for slot in slots[1:]:
            more = jnp.logical_or(more, wants(slot, n, worsts[slot]))
        return more

    def body(state):
        n, worsts = state
        new = []
        for slot in slots:
            new.append(lax.cond(
                wants(slot, n, worsts[slot]),
                lambda slot=slot: sweep([(slot, [(key_start(slot, n), None)])], first=False)[0],
                lambda slot=slot: worsts[slot]))
        return n + 1, tuple(new)

    lax.while_loop(more_blocks, body, (jnp.int32(2), worsts))
    for slot in slots:
        q_rows = slice(slot * blk, (slot + 1) * blk)
        for g in range(heads):
            lanes = slice(g * dh, (g + 1) * dh)
            a_ref[q_rows, lanes] = (acc_ref[slot, g]
                                    * sz_ref[q_rows, lanes].astype(F32)).astype(a_ref.dtype)


def _sb_attention(proj, *, batch, seq, blk=ATTN_BLOCK, heads=ATTN_HEADS_PER_STEP):
    m = proj.shape[0]
    dh = SB_HEAD_DIM
    tile_rows = blk * ATTN_QUERY_BLOCKS
    nq = seq // tile_rows
    groups = SB_HEADS // heads
    width = heads * dh
    return pl.pallas_call(
        functools.partial(_sb_attn_kernel, blk=blk, heads=heads),
        grid=(batch, groups, nq),
        in_specs=[
            pl.BlockSpec((tile_rows, width), lambda b, h, i: (b * nq + i, h)),
            pl.BlockSpec((seq, width), lambda b, h, i: (b, groups + h)),
            pl.BlockSpec((seq, width), lambda b, h, i: (b, 2 * groups + h)),
            pl.BlockSpec((tile_rows, width), lambda b, h, i: (b * nq + i, 3 * groups + h)),
        ],
        out_specs=pl.BlockSpec((tile_rows, width), lambda b, h, i: (b * nq + i, h)),
        out_shape=jax.ShapeDtypeStruct((m, BRANCH), BF16),
        scratch_shapes=[pltpu.VMEM((ATTN_QUERY_BLOCKS, heads, blk, dh), F32),
                        pltpu.VMEM((ATTN_QUERY_BLOCKS, heads, blk, dh), F32)],
        compiler_params=_params("parallel", "parallel", "arbitrary"),
        name="sb_attention",
    )(proj, proj, proj, proj)


OUTPROJ_ROW_CHUNKS = 4


def _outproj_kernel(a_ref, w32_ref, h_ref, gpost_ref, gpre_ref, hout_ref, u_ref, w_ref):
    _stage_weights(pl.program_id(0) == 0, [(w32_ref, w_ref)])
    rows = a_ref.shape[0] // OUTPROJ_ROW_CHUNKS
    for c in range(OUTPROJ_ROW_CHUNKS):
        r = pl.ds(c * rows, rows)
        hn = h_ref[r, :] + _rms(_mm(a_ref[r, :], w_ref), gpost_ref[...])
        hout_ref[r, :] = hn
        u_ref[r, :] = _rms(hn, gpre_ref[...]).astype(u_ref.dtype)


def _outproj_last_kernel(a_ref, w32_ref, h_ref, gpost_ref, hout_ref, w_ref):
    _stage_weights(pl.program_id(0) == 0, [(w32_ref, w_ref)])
    hout_ref[...] = h_ref[...] + _rms(_mm(a_ref[...], w_ref), gpost_ref[...])


def _outproj(a, w_out, h, g_post, g_pre_next, *, tm=512):
    m, d = h.shape
    row = pl.BlockSpec((tm, d), lambda i: (i, 0))
    gain = pl.BlockSpec((1, d), lambda i: (0, 0))
    w_spec = pl.BlockSpec((BRANCH, d), lambda i: (0, 0), pipeline_mode=pl.Buffered(1))
    w_scratch = [pltpu.VMEM((BRANCH, d), BF16)]
    if g_pre_next is None:
        return pl.pallas_call(
            _outproj_last_kernel,
            grid=(m // tm,),
            in_specs=[row, w_spec, row, gain],
            out_specs=row,
            out_shape=jax.ShapeDtypeStruct((m, d), F32),
            scratch_shapes=w_scratch,
            compiler_params=_params("arbitrary"),
            name="outproj_last",
        )(a, w_out, h, g_post), None
    return pl.pallas_call(
        _outproj_kernel,
        grid=(m // tm,),
        in_specs=[row, w_spec, row, gain, gain],
        out_specs=[row, row],
        out_shape=[jax.ShapeDtypeStruct((m, d), F32),
                   jax.ShapeDtypeStruct((m, d), BF16)],
        scratch_shapes=w_scratch,
        compiler_params=_params("arbitrary"),
        name="outproj",
    )(a, w_out, h, g_post, g_pre_next)


def kernel(x,
           ln_pre_0, conv_w_in_0, conv_w_0, conv_w_out_0, ln_post_0,
           ln_pre_1, sb_w_in_1, sb_w_out_1, ln_post_1,
           ln_pre_2, conv_w_in_2, conv_w_2, conv_w_out_2, ln_post_2,
           ln_pre_3, sb_w_in_3, sb_w_out_3, ln_post_3):
    batch, seq, d = x.shape
    gain = lambda g: g.reshape(1, d).astype(F32)
    layers = [
        (ln_pre_0, (conv_w_in_0, conv_w_0, conv_w_out_0), ln_post_0),
        (ln_pre_1, (sb_w_in_1, sb_w_out_1), ln_post_1),
        (ln_pre_2, (conv_w_in_2, conv_w_2, conv_w_out_2), ln_post_2),
        (ln_pre_3, (sb_w_in_3, sb_w_out_3), ln_post_3),
    ]
    h = x.reshape(batch * seq, d)
    u = _prenorm(h, gain(layers[0][0]))
    for idx, (_, params, g_post) in enumerate(layers):
        if len(params) == 3:
            w_in, conv_w, w_out = params
            a = _conv_mixer(u, w_in, conv_w, seq=seq)
        else:
            w_in, w_out = params
            proj = _sb_inproj(u, w_in)
            a = _sb_attention(proj, batch=batch, seq=seq)
        g_pre_next = gain(layers[idx + 1][0]) if idx + 1 < len(layers) else None
        h, u = _outproj(a, w_out, h, gain(g_post), g_pre_next)
    return h.reshape(batch, seq, d)
```

```python
import functools
import math

import jax
import jax.numpy as jnp
from jax import lax
from jax.experimental import pallas as pl
from jax.experimental.pallas import tpu as pltpu

D_MODEL = 2048
BRANCH = D_MODEL
CONV_K = 3
SB_HEADS = 16
SB_HEAD_DIM = BRANCH // SB_HEADS
RMS_EPS = 1e-6

F32_ROW_TILE = 8
VMEM_LIMIT_BYTES = 56 * 1024 * 1024

F32 = jnp.float32
BF16 = jnp.bfloat16
LOG2E = math.log2(math.e)
LOG2_WEIGHT_UNDERFLOW = float("-inf")


def _params(*semantics):
    return pltpu.CompilerParams(dimension_semantics=semantics,
                                vmem_limit_bytes=VMEM_LIMIT_BYTES)


def _rms(x, g):
    ms = jnp.mean(x * x, axis=-1, keepdims=True)
    return x * lax.rsqrt(ms + RMS_EPS) * g


def _silu(z):
    return z * jax.nn.sigmoid(z)


def _mm(a, w_ref):
    return jnp.dot(a, w_ref[...], preferred_element_type=F32)


def _suffix_sum(x, suffix):
    return jnp.dot(x.astype(BF16), suffix, preferred_element_type=F32)


def _stage_weights(first_use, pairs):
    @pl.when(first_use)
    def _():
        for w_ref, wb_ref in pairs:
            wb_ref[...] = w_ref[...].astype(wb_ref.dtype)


def _prenorm_kernel(x_ref, g_ref, u_ref):
    u_ref[...] = _rms(x_ref[...], g_ref[...]).astype(u_ref.dtype)


def _prenorm(x, g, *, tm=512):
    m, d = x.shape
    return pl.pallas_call(
        _prenorm_kernel,
        grid=(m // tm,),
        in_specs=[pl.BlockSpec((tm, d), lambda i: (i, 0)),
                  pl.BlockSpec((1, d), lambda i: (0, 0))],
        out_specs=pl.BlockSpec((tm, d), lambda i: (i, 0)),
        out_shape=jax.ShapeDtypeStruct((m, d), BF16),
        compiler_params=_params("parallel"),
        name="prenorm",
    )(x, g)


ROW_SUBTILES = 2


def _conv_mixer_kernel(u_ref, wb32_ref, wc32_ref, wx32_ref, wz32_ref, cw_ref,
                       a_ref, wb_ref, wc_ref, wx_ref, wz_ref, tail_ref, *, seq):
    i = pl.program_id(1)
    _stage_weights(i == 0, [(wb32_ref, wb_ref), (wc32_ref, wc_ref),
                            (wx32_ref, wx_ref), (wz32_ref, wz_ref)])

    @pl.when(i == 0)
    def _():
        tail_ref[...] = jnp.zeros_like(tail_ref)

    tm = u_ref.shape[0]
    sub = tm // ROW_SUBTILES
    cw = cw_ref[...]
    tail = tail_ref[...]
    for r in range(ROW_SUBTILES):
        u = u_ref[pl.ds(r * sub, sub), :]
        cx = _mm(u, wc_ref) * _mm(u, wx_ref)
        tail = jnp.where((i * tm + r * sub) % seq == 0, 0.0, tail)
        full = jnp.concatenate([tail, cx], axis=0)
        prev1 = pltpu.roll(full, 1, 0)[F32_ROW_TILE:]
        prev2 = pltpu.roll(full, 2, 0)[F32_ROW_TILE:]
        conv = cw[2:3] * cx + cw[1:2] * prev1 + cw[0:1] * prev2
        y = _mm(u, wb_ref) * conv
        z = _mm(u, wz_ref)
        a_ref[pl.ds(r * sub, sub), :] = (_silu(z) * y).astype(a_ref.dtype)
        tail = cx[sub - F32_ROW_TILE:]
    tail_ref[...] = tail


def _conv_mixer(u, w_in, conv_w, *, seq, tm=2048, tn=256):
    m, d = u.shape
    nj = BRANCH // tn
    assert seq % (tm // ROW_SUBTILES) == 0, "a sequence must start on a row sub-tile boundary"

    def w_spec(chunk):
        return pl.BlockSpec((d, tn), lambda j, i: (0, chunk * nj + j))

    return pl.pallas_call(
        functools.partial(_conv_mixer_kernel, seq=seq),
        grid=(nj, m // tm),
        in_specs=[
            pl.BlockSpec((tm, d), lambda j, i: (i, 0)),
            w_spec(0), w_spec(1), w_spec(2), w_spec(3),
            pl.BlockSpec((CONV_K, tn), lambda j, i: (0, j)),
        ],
        out_specs=pl.BlockSpec((tm, tn), lambda j, i: (i, j)),
        out_shape=jax.ShapeDtypeStruct((m, BRANCH), BF16),
        scratch_shapes=[pltpu.VMEM((d, tn), BF16)] * 4 + [pltpu.VMEM((F32_ROW_TILE, tn), F32)],
        compiler_params=_params("arbitrary", "arbitrary"),
        name="conv_mixer",
    )(u, w_in, w_in, w_in, w_in, conv_w)


INPROJ_COL_CHUNK = 256


def _sb_inproj_kernel(u_ref, w32_ref, p_ref, w_ref, *, q_tiles, z_tile0, scale):
    j = pl.program_id(0)
    _stage_weights(pl.program_id(1) == 0, [(w32_ref, w_ref)])
    sub = u_ref.shape[0] // ROW_SUBTILES
    col_scale = jnp.where(j < q_tiles, scale, 1.0).astype(F32)

    def project(epilogue):
        for r in range(ROW_SUBTILES):
            rows = pl.ds(r * sub, sub)
            u = u_ref[rows, :]
            for c in range(p_ref.shape[1] // INPROJ_COL_CHUNK):
                cols = pl.ds(c * INPROJ_COL_CHUNK, INPROJ_COL_CHUNK)
                acc = jnp.dot(u, w_ref[:, cols], preferred_element_type=F32)
                p_ref[rows, cols] = epilogue(acc).astype(p_ref.dtype)

    @pl.when(j >= z_tile0)
    def _():
        project(_silu)

    @pl.when(j < z_tile0)
    def _():
        project(lambda acc: acc * col_scale)


def _sb_inproj(u, w_in, *, tm=2048, tn=1024):
    m, d = u.shape
    n = w_in.shape[1]
    chunk_tiles = BRANCH // tn
    return pl.pallas_call(
        functools.partial(_sb_inproj_kernel, q_tiles=chunk_tiles,
                          z_tile0=3 * chunk_tiles,
                          scale=LOG2E / math.sqrt(SB_HEAD_DIM)),
        grid=(n // tn, m // tm),
        in_specs=[pl.BlockSpec((tm, d), lambda j, i: (i, 0)),
                  pl.BlockSpec((d, tn), lambda j, i: (0, j))],
        out_specs=pl.BlockSpec((tm, tn), lambda j, i: (i, j)),
        out_shape=jax.ShapeDtypeStruct((m, n), BF16),
        scratch_shapes=[pltpu.VMEM((d, tn), BF16)],
        compiler_params=_params("arbitrary", "arbitrary"),
        name="sb_inproj",
    )(u, w_in)


ATTN_BLOCK = 256
ATTN_HEADS_PER_STEP = 8
ATTN_QUERY_BLOCKS = 2


def _sb_attn_kernel(q_ref, k_ref, v_ref, sz_ref, a_ref, acc_ref, carry_ref, *,
                    blk, heads):
    tile = pl.program_id(2)
    dh = SB_HEAD_DIM
    lane_reps = blk // dh
    rows = lax.broadcasted_iota(jnp.int32, (blk, blk), 0)
    cols = lax.broadcasted_iota(jnp.int32, (blk, blk), 1)
    suffix = (rows > cols).astype(BF16)
    causal = cols < rows

    def query_block(slot):
        return tile * ATTN_QUERY_BLOCKS + slot

    def key_start(slot, n):
        return pl.multiple_of((query_block(slot) - n) * blk, blk)

    def sweep(work, first):
        worsts = []
        for slot, blocks in work:
            q_rows = slice(slot * blk, (slot + 1) * blk)
            worst = None
            for g in range(heads):
                lanes = slice(g * dh, (g + 1) * dh)
                q = q_ref[q_rows, lanes]
                carry = jnp.zeros((blk, dh), F32) if first else carry_ref[slot, g]
                acc = jnp.zeros((blk, dh), F32) if first else acc_ref[slot, g]
                gates = []
                for start, mask in blocks:
                    l2 = lax.dot_general(q, k_ref[pl.ds(start, blk), lanes],
                                         (((1,), (1,)), ((), ())),
                                         preferred_element_type=F32)
                    t2 = jnp.log(1.0 + jnp.exp2(-jnp.abs(l2))) * LOG2E
                    log_beta = jnp.minimum(l2, 0.0) - t2
                    log_keep = log_beta - l2
                    if mask is not None:
                        log_keep = jnp.where(mask, log_keep, 0.0)
                    gates.append((log_beta, log_keep))
                tails = _suffix_sum(jnp.concatenate([keep for _, keep in gates], axis=0), suffix)
                for idx, (start, mask) in enumerate(blocks):
                    log_beta, log_keep = gates[idx]
                    tail = tails[idx * blk:(idx + 1) * blk]
                    w = jnp.exp2(log_beta + jnp.concatenate([carry] * lane_reps, axis=1) + tail)
                    if mask is not None:
                        w = jnp.where(mask, w, 0.0)
                    acc = acc + _mm(w.astype(BF16), v_ref.at[pl.ds(start, blk), lanes])
                    carry = carry + jnp.sum(log_keep, axis=-1, keepdims=True)
                acc_ref[slot, g] = acc
                carry_ref[slot, g] = carry
                worst = carry if worst is None else jnp.maximum(worst, carry)
            worsts.append(jnp.max(worst))
        return tuple(worsts)

    slots = range(ATTN_QUERY_BLOCKS)

    def first_sweep(first_has_left):
        work = []
        for slot in slots:
            blocks = [(key_start(slot, 0), causal)]
            if slot > 0 or first_has_left:
                blocks.append((key_start(slot, 1), None))
            work.append((slot, blocks))
        return sweep(work, first=True)

    worsts = lax.cond(tile == 0, lambda: first_sweep(False), lambda: first_sweep(True))

    def wants(slot, n, worst):
        return jnp.logical_and(n <= query_block(slot), worst > LOG2_WEIGHT_UNDERFLOW)

    def more_blocks(state):
        n, worsts = state
        more = wants(0, n, worsts[0])
        for slot in slots[1:]:
            more = jnp.logical_or(more, wants(slot, n, worsts[slot]))
        return more

    def body(state):
        n, worsts = state
        new = []
        for slot in slots:
            new.append(lax.cond(
                wants(slot, n, worsts[slot]),
                lambda slot=slot: sweep([(slot, [(key_start(slot, n), None)])], first=False)[0],
                lambda slot=slot: worsts[slot]))
        return n + 1, tuple(new)

    lax.while_loop(more_blocks, body, (jnp.int32(2), worsts))
    for slot in slots:
        q_rows = slice(slot * blk, (slot + 1) * blk)
        for g in range(heads):
            lanes = slice(g * dh, (g + 1) * dh)
            a_ref[q_rows, lanes] = (acc_ref[slot, g]
                                    * sz_ref[q_rows, lanes].astype(F32)).astype(a_ref.dtype)


def _sb_attention(proj, *, batch, seq, blk=ATTN_BLOCK, heads=ATTN_HEADS_PER_STEP):
    m = proj.shape[0]
    dh = SB_HEAD_DIM
    tile_rows = blk * ATTN_QUERY_BLOCKS
    nq = seq // tile_rows
    groups = SB_HEADS // heads
    width = heads * dh
    return pl.pallas_call(
        functools.partial(_sb_attn_kernel, blk=blk, heads=heads),
        grid=(batch, groups, nq),
        in_specs=[
            pl.BlockSpec((tile_rows, width), lambda b, h, i: (b * nq + i, h)),
            pl.BlockSpec((seq, width), lambda b, h, i: (b, groups + h)),
            pl.BlockSpec((seq, width), lambda b, h, i: (b, 2 * groups + h)),
            pl.BlockSpec((tile_rows, width), lambda b, h, i: (b * nq + i, 3 * groups + h)),
        ],
        out_specs=pl.BlockSpec((tile_rows, width), lambda b, h, i: (b * nq + i, h)),
        out_shape=jax.ShapeDtypeStruct((m, BRANCH), BF16),
        scratch_shapes=[pltpu.VMEM((ATTN_QUERY_BLOCKS, heads, blk, dh), F32),
                        pltpu.VMEM((ATTN_QUERY_BLOCKS, heads, blk, dh), F32)],
        compiler_params=_params("parallel", "parallel", "arbitrary"),
        name="sb_attention",
    )(proj, proj, proj, proj)


OUTPROJ_ROW_CHUNKS = 4


def _outproj_kernel(a_ref, w32_ref, h_ref, gpost_ref, gpre_ref, hout_ref, u_ref, w_ref):
    _stage_weights(pl.program_id(0) == 0, [(w32_ref, w_ref)])
    rows = a_ref.shape[0] // OUTPROJ_ROW_CHUNKS
    for c in range(OUTPROJ_ROW_CHUNKS):
        r = pl.ds(c * rows, rows)
        hn = h_ref[r, :] + _rms(_mm(a_ref[r, :], w_ref), gpost_ref[...])
        hout_ref[r, :] = hn
        u_ref[r, :] = _rms(hn, gpre_ref[...]).astype(u_ref.dtype)


def _outproj_last_kernel(a_ref, w32_ref, h_ref, gpost_ref, hout_ref, w_ref):
    _stage_weights(pl.program_id(0) == 0, [(w32_ref, w_ref)])
    hout_ref[...] = h_ref[...] + _rms(_mm(a_ref[...], w_ref), gpost_ref[...])


def _outproj(a, w_out, h, g_post, g_pre_next, *, tm=512):
    m, d = h.shape
    row = pl.BlockSpec((tm, d), lambda i: (i, 0))
    gain = pl.BlockSpec((1, d), lambda i: (0, 0))
    w_spec = pl.BlockSpec((BRANCH, d), lambda i: (0, 0), pipeline_mode=pl.Buffered(1))
    w_scratch = [pltpu.VMEM((BRANCH, d), BF16)]
    if g_pre_next is None:
        return pl.pallas_call(
            _outproj_last_kernel,
            grid=(m // tm,),
            in_specs=[row, w_spec, row, gain],
            out_specs=row,
            out_shape=jax.ShapeDtypeStruct((m, d), F32),
            scratch_shapes=w_scratch,
            compiler_params=_params("arbitrary"),
            name="outproj_last",
        )(a, w_out, h, g_post), None
    return pl.pallas_call(
        _outproj_kernel,
        grid=(m // tm,),
        in_specs=[row, w_spec, row, gain, gain],
        out_specs=[row, row],
        out_shape=[jax.ShapeDtypeStruct((m, d), F32),
                   jax.ShapeDtypeStruct((m, d), BF16)],
        scratch_shapes=w_scratch,
        compiler_params=_params("arbitrary"),
        name="outproj",
    )(a, w_out, h, g_post, g_pre_next)


def kernel(x,
           ln_pre_0, conv_w_in_0, conv_w_0, conv_w_out_0, ln_post_0,
           ln_pre_1, sb_w_in_1, sb_w_out_1, ln_post_1,
           ln_pre_2, conv_w_in_2, conv_w_2, conv_w_out_2, ln_post_2,
           ln_pre_3, sb_w_in_3, sb_w_out_3, ln_post_3):
    batch, seq, d = x.shape
    gain = lambda g: g.reshape(1, d).astype(F32)
    layers = [
        (ln_pre_0, (conv_w_in_0, conv_w_0, conv_w_out_0), ln_post_0),
        (ln_pre_1, (sb_w_in_1, sb_w_out_1), ln_post_1),
        (ln_pre_2, (conv_w_in_2, conv_w_2, conv_w_out_2), ln_post_2),
        (ln_pre_3, (sb_w_in_3, sb_w_out_3), ln_post_3),
    ]
    h = x.reshape(batch * seq, d)
    u = _prenorm(h, gain(layers[0][0]))
    for idx, (_, params, g_post) in enumerate(layers):
        if len(params) == 3:
            w_in, conv_w, w_out = params
            a = _conv_mixer(u, w_in, conv_w, seq=seq)
        else:
            w_in, w_out = params
            proj = _sb_inproj(u, w_in)
            a = _sb_attention(proj, batch=batch, seq=seq)
        g_pre_next = gain(layers[idx + 1][0]) if idx + 1 < len(layers) else None
        h, u = _outproj(a, w_out, h, gain(g_post), g_pre_next)
    return h.reshape(batch, seq, d)
```
